```python
import math
import jax, jax.numpy as jnp
from jax import lax
import numpy as np

D_MODEL = 2048
BATCH = 2
SEQ = 4096
DEPTH = 1

CHUNK = 64
D_MIX = D_MODEL
SGU_WIDTH = D_MIX // 2
SGU_HEADS = 8
SGU_HEAD_DIM = SGU_WIDTH // SGU_HEADS
SGU_BLOCK = 128
RET_WIDTH = D_MIX - SGU_WIDTH
RET_HEADS = 8
RET_V_DIM = RET_WIDTH // RET_HEADS
RET_QK_DIM = RET_V_DIM // 2
RET_QK_WIDTH = RET_HEADS * RET_QK_DIM
ROPE_BASE = 10000.0
D_FF = 5632
CONV_WIDTH = 3
EPS = 1e-6
IN_COLS = 2 * SGU_WIDTH + 2 * RET_QK_WIDTH + 2 * RET_WIDTH
SPLITS = (SGU_WIDTH, 2 * SGU_WIDTH, 2 * SGU_WIDTH + RET_QK_WIDTH,
          2 * SGU_WIDTH + 2 * RET_QK_WIDTH, 2 * SGU_WIDTH + 2 * RET_QK_WIDTH + RET_WIDTH)

kernel_name = "hybrid_sgu_retention_convffn"


def rmsnorm(x, g):
    x32 = x.astype(jnp.float32)
    y = x32 * lax.rsqrt(jnp.mean(x32 * x32, axis=-1, keepdims=True) + EPS)
    return (y * g.astype(jnp.float32)).astype(x.dtype)


def layernorm(x, g, b):
    x32 = x.astype(jnp.float32)
    mu = jnp.mean(x32, axis=-1, keepdims=True)
    var = jnp.mean(jnp.square(x32 - mu), axis=-1, keepdims=True)
    y = (x32 - mu) * lax.rsqrt(var + EPS)
    return (y * g.astype(jnp.float32) + b.astype(jnp.float32)).astype(x.dtype)


def rotary(x):
    s, d = x.shape[1], x.shape[-1]
    inv_freq = ROPE_BASE ** (-jnp.arange(0, d, 2, dtype=jnp.float32) / d)
    ang = jnp.arange(s, dtype=jnp.float32)[:, None] * inv_freq[None, :]
    cos = jnp.cos(ang)[None, :, None, :]
    sin = jnp.sin(ang)[None, :, None, :]
    x1, x2 = jnp.split(x.astype(jnp.float32), 2, axis=-1)
    out = jnp.concatenate([x1 * cos - x2 * sin, x1 * sin + x2 * cos], axis=-1)
    return out.astype(x.dtype)


def spatial_gating(u, v, w_s, b_s, ln_g, ln_b):
    bsz, s, _ = v.shape
    v = layernorm(v, ln_g, ln_b)
    nb = s // SGU_BLOCK
    vb = v.reshape(bsz, nb, SGU_BLOCK, SGU_HEADS, SGU_HEAD_DIM)
    chunk_id = jnp.arange(SGU_BLOCK) // CHUNK
    mask = chunk_id[:, None] >= chunk_id[None, :]
    w = jnp.where(mask[None], w_s, jnp.zeros_like(w_s))
    mixed = jnp.einsum('hts,bnshc->bnthc', w, vb) + jnp.transpose(b_s)[None, None, :, :, None]
    return u * mixed.reshape(bsz, s, SGU_WIDTH)


def retention(q, k, v):
    bsz, s, h, dk = q.shape
    dv = v.shape[-1]
    nc = s // CHUNK
    gammas = 1.0 - jnp.exp2(-5.0 - jnp.arange(h, dtype=jnp.float32))
    log_g = jnp.log(gammas)
    idx = jnp.arange(CHUNK, dtype=jnp.float32)
    d_in = jnp.exp(log_g[:, None, None] * jnp.abs(idx[:, None] - idx[None, :]))
    qc = q.reshape(bsz, nc, CHUNK, h, dk)
    kc = k.reshape(bsz, nc, CHUNK, h, dk)
    vc = v.reshape(bsz, nc, CHUNK, h, dv)
    scores = jnp.einsum('bcnhd,bcmhd->bchnm', qc, kc) * d_in[None, None]
    inner = jnp.einsum('bchnm,bcmhe->bcnhe', scores, vc)
    k_dec = jnp.exp(log_g[None, :] * (CHUNK - 1 - idx)[:, None])
    kv = jnp.einsum('bcmhd,mh,bcmhe->cbhde', kc, k_dec, vc)
    chunk_decay = jnp.exp(log_g * CHUNK)[:, None, None]

    def step(state, kv_c):
        return state * chunk_decay + kv_c, state

    init = jnp.zeros((bsz, h, dk, dv), kv.dtype)
    _, prev = lax.scan(step, init, kv)
    q_dec = jnp.exp(log_g[None, :] * (idx + 1.0)[:, None])
    cross = jnp.einsum('bcnhd,nh,cbhde->bcnhe', qc, q_dec, prev)
    return (inner + cross).reshape(bsz, s, h, dv)


def group_norm_heads(y, g):
    y32 = y.astype(jnp.float32)
    mu = jnp.mean(y32, axis=-1, keepdims=True)
    var = jnp.mean(jnp.square(y32 - mu), axis=-1, keepdims=True)
    yn = ((y32 - mu) * lax.rsqrt(var + EPS)).reshape(y.shape[0], y.shape[1], -1)
    return (yn * g.astype(jnp.float32)).astype(y.dtype)


def conv_ffn(h, w_up, conv_w, conv_b, w_down):
    up = h @ w_up
    c = up.shape[-1]
    conv = lax.conv_general_dilated(
        up, conv_w.reshape(CONV_WIDTH, 1, c).astype(up.dtype),
        window_strides=(1,), padding=[(CONV_WIDTH - 1, 0)],
        dimension_numbers=('NWC', 'WIO', 'NWC'), feature_group_count=c) + conv_b
    gate, val = jnp.split(conv, 2, axis=-1)
    return (jax.nn.silu(gate) * val) @ w_down


def setup_inputs(seed: int = 0) -> dict:
    key = jax.random.key(seed)
    ks = jax.random.split(key, 16)
    f32 = jnp.float32
    nrm = lambda k, shape, scale: jax.random.normal(k, shape, f32) * scale
    x = jax.random.normal(ks[0], (BATCH, SEQ, D_MODEL), f32)
    norm1_g = 1.0 + nrm(ks[1], (DEPTH, D_MODEL), 0.02)
    w_in = nrm(ks[2], (DEPTH, D_MODEL, IN_COLS), D_MODEL ** -0.5)
    sgu_ln_g = 1.0 + nrm(ks[3], (DEPTH, SGU_WIDTH), 0.02)
    sgu_ln_b = nrm(ks[4], (DEPTH, SGU_WIDTH), 0.02)
    sgu_w_s = nrm(ks[5], (DEPTH, SGU_HEADS, SGU_BLOCK, SGU_BLOCK), SGU_BLOCK ** -0.5)
    sgu_b_s = 1.0 + nrm(ks[6], (DEPTH, SGU_HEADS, SGU_BLOCK), 0.02)
    ret_gn_g = 1.0 + nrm(ks[7], (DEPTH, RET_WIDTH), 0.02)
    w_out = nrm(ks[8], (DEPTH, D_MIX, D_MODEL), D_MIX ** -0.5)
    norm2_g = 1.0 + nrm(ks[9], (DEPTH, D_MODEL), 0.02)
    w_up = nrm(ks[10], (DEPTH, D_MODEL, 2 * D_FF), D_MODEL ** -0.5)
    conv_w = nrm(ks[11], (DEPTH, CONV_WIDTH, 2 * D_FF), CONV_WIDTH ** -0.5)
    conv_b = nrm(ks[12], (DEPTH, 2 * D_FF), 0.02)
    w_down = nrm(ks[13], (DEPTH, D_FF, D_MODEL), D_FF ** -0.5)
    final_g = 1.0 + nrm(ks[14], (D_MODEL,), 0.02)
    return {"x": x, "norm1_g": norm1_g, "w_in": w_in, "sgu_ln_g": sgu_ln_g,
            "sgu_ln_b": sgu_ln_b, "sgu_w_s": sgu_w_s, "sgu_b_s": sgu_b_s,
            "ret_gn_g": ret_gn_g, "w_out": w_out, "norm2_g": norm2_g, "w_up": w_up,
            "conv_w": conv_w, "conv_b": conv_b, "w_down": w_down, "final_g": final_g}


def reference(x, norm1_g, w_in, sgu_ln_g, sgu_ln_b, sgu_w_s, sgu_b_s, ret_gn_g,
              w_out, norm2_g, w_up, conv_w, conv_b, w_down, final_g):
    bsz, s, _ = x.shape
    for l in range(DEPTH):
        h = rmsnorm(x, norm1_g[l])
        z = h @ w_in[l]
        zu, zv, q, k, v, g = jnp.split(z, SPLITS, axis=-1)
        a_out = spatial_gating(jax.nn.gelu(zu), jax.nn.gelu(zv), sgu_w_s[l], sgu_b_s[l],
                               sgu_ln_g[l], sgu_ln_b[l])
        q = rotary(q.reshape(bsz, s, RET_HEADS, RET_QK_DIM))
        k = rotary(k.reshape(bsz, s, RET_HEADS, RET_QK_DIM)) * (RET_QK_DIM ** -0.5)
        v = v.reshape(bsz, s, RET_HEADS, RET_V_DIM)
        y = group_norm_heads(retention(q, k, v), ret_gn_g[l])
        b_out = jax.nn.silu(g) * y
        mix = jnp.concatenate([a_out, b_out], axis=-1) @ w_out[l]
        x = x + mix
        x = x + conv_ffn(rmsnorm(x, norm2_g[l]), w_up[l], conv_w[l], conv_b[l], w_down[l])
    return rmsnorm(x, final_g)
```

```python
import functools

import jax
import jax.numpy as jnp
from jax import lax
from jax.experimental import pallas as pl
from jax.experimental.pallas import tpu as pltpu

F32 = jnp.float32
BF16 = jnp.bfloat16

D_MODEL = 2048
CHUNK = 64
SGU_WIDTH = 1024
SGU_HEADS = 8
SGU_BLOCK = 128
RET_WIDTH = 1024
RET_HEADS = 8
RET_V_DIM = 128
RET_QK_DIM = 64
RET_QK_WIDTH = 512
ROPE_BASE = 10000.0
D_FF = 5632
EPS = 1e-6
IN_COLS = 2 * SGU_WIDTH + 2 * RET_QK_WIDTH + 2 * RET_WIDTH
LANES = 128

VMEM_LIMIT_BYTES = 56 * 1024 * 1024


def _params(n_axes):
    return pltpu.CompilerParams(
        dimension_semantics=("arbitrary",) * n_axes,
        vmem_limit_bytes=VMEM_LIMIT_BYTES,
    )


def _rms(x, g):
    ms = jnp.mean(x * x, axis=-1, keepdims=True)
    return x * lax.rsqrt(ms + EPS) * g


def _norm_kernel(x_ref, g_ref, o_ref):
    o_ref[...] = _rms(x_ref[...], g_ref[...]).astype(o_ref.dtype)


def _norm(x2d, g, bm=512):
    m, d = x2d.shape
    return pl.pallas_call(
        _norm_kernel,
        grid=(m // bm,),
        in_specs=[pl.BlockSpec((bm, d), lambda i: (i, 0)),
                  pl.BlockSpec((1, d), lambda i: (0, 0))],
        out_specs=pl.BlockSpec((bm, d), lambda i: (i, 0)),
        out_shape=jax.ShapeDtypeStruct((m, d), BF16),
        compiler_params=_params(1),
        name="norm1",
    )(x2d, g.reshape(1, d))


def _inproj_kernel(h_ref, w_ref, lng_ref, lnb_ref, cos_ref, sin_ref, o_ref):
    j = pl.program_id(0)
    acc = jnp.dot(h_ref[...], w_ref[...], preferred_element_type=F32)

    @pl.when(j == 0)
    def _():
        o_ref[...] = jax.nn.gelu(acc).astype(o_ref.dtype)

    @pl.when(j == 1)
    def _():
        a = jax.nn.gelu(acc)
        mu = jnp.mean(a, axis=-1, keepdims=True)
        c = a - mu
        var = jnp.mean(c * c, axis=-1, keepdims=True)
        y = c * lax.rsqrt(var + EPS) * lng_ref[...] + lnb_ref[...]
        o_ref[...] = y.astype(o_ref.dtype)

    @pl.when(j == 2)
    def _():
        cos = cos_ref[...]
        sin = sin_ref[...]
        lane = lax.broadcasted_iota(jnp.int32, cos.shape, 1)
        first_half = (lane % RET_QK_DIM) < (RET_QK_DIM // 2)
        n_col = o_ref.shape[1] // LANES
        for c in range(n_col):
            xc = acc[:, c * LANES:(c + 1) * LANES]
            partner = jnp.where(first_half,
                                pltpu.roll(xc, LANES - RET_QK_DIM // 2, 1),
                                pltpu.roll(xc, RET_QK_DIM // 2, 1))
            r = xc * cos + partner * sin
            if c >= n_col // 2:
                r = r * (RET_QK_DIM ** -0.5)
            o_ref[:, c * LANES:(c + 1) * LANES] = r.astype(o_ref.dtype)

    @pl.when(j == 3)
    def _():
        o_ref[...] = acc.astype(o_ref.dtype)

    @pl.when(j == 4)
    def _():
        o_ref[...] = jax.nn.silu(acc).astype(o_ref.dtype)


def _in_proj(h1, w_in, ln_g, ln_b, cos_t, sin_t, seq, bm=1024, bn=1024):
    m, k = h1.shape
    n = w_in.shape[1]
    seq_tiles = seq // bm
    return pl.pallas_call(
        _inproj_kernel,
        grid=(n // bn, m // bm),
        in_specs=[pl.BlockSpec((bm, k), lambda j, i: (i, 0)),
                  pl.BlockSpec((k, bn), lambda j, i: (0, j)),
                  pl.BlockSpec((1, bn), lambda j, i: (0, 0)),
                  pl.BlockSpec((1, bn), lambda j, i: (0, 0)),
                  pl.BlockSpec((bm, LANES), lambda j, i: (i % seq_tiles, 0)),
                  pl.BlockSpec((bm, LANES), lambda j, i: (i % seq_tiles, 0))],
        out_specs=pl.BlockSpec((bm, bn), lambda j, i: (i, j)),
        out_shape=jax.ShapeDtypeStruct((m, n), BF16),
        compiler_params=_params(2),
        name="in_proj",
    )(h1, w_in, ln_g.reshape(1, bn), ln_b.reshape(1, bn), cos_t, sin_t)


def _mixer_kernel(gu_ref, vn_ref, qk_ref, vr_ref, gate_ref, ws_ref, bs_ref,
                  dmat_ref, qdec_ref, kdec_ref, cdec_ref, gng_ref,
                  o_ref, state_ref, *, steps_per_seq, blocks_per_step):
    step = pl.program_id(0)

    @pl.when(step % steps_per_seq == 0)
    def _():
        state_ref[...] = jnp.zeros_like(state_ref)

    t = SGU_BLOCK
    row_i = lax.broadcasted_iota(jnp.int32, (t, t), 0)
    col_i = lax.broadcasted_iota(jnp.int32, (t, t), 1)
    sgu_mask = (row_i // CHUNK) >= (col_i // CHUNK)
    lane_lo = col_i < RET_QK_DIM
    row_lo = row_i < RET_QK_DIM

    def block(r, carry):
        rows = pl.ds(pl.multiple_of(r * t, t), t)

        for h in range(SGU_HEADS):
            cols = slice(h * t, (h + 1) * t)
            w = jnp.where(sgu_mask, ws_ref[h], jnp.zeros((), ws_ref.dtype))
            mixed = jnp.dot(w, vn_ref[rows, cols], preferred_element_type=F32)
            mixed = mixed + bs_ref[:, h:h + 1]
            o_ref[rows, cols] = (gu_ref[rows, cols].astype(F32) * mixed).astype(o_ref.dtype)

        for p in range(RET_HEADS // 2):
            qp = qk_ref[rows, p * LANES:(p + 1) * LANES]
            kp = qk_ref[rows, RET_QK_WIDTH + p * LANES:RET_QK_WIDTH + (p + 1) * LANES]
            vp = vr_ref[rows, 2 * p * RET_V_DIM:2 * (p + 1) * RET_V_DIM]
            st = state_ref[p]
            st_b = st.astype(BF16)
            for e in range(2):
                h = 2 * p + e
                keep = lane_lo if e == 0 else jnp.logical_not(lane_lo)
                qm = jnp.where(keep, qp, jnp.zeros((), qp.dtype))
                s = lax.dot_general(qm, kp, (((1,), (1,)), ((), ())),
                                    preferred_element_type=F32)
                s = (s * dmat_ref[h]).astype(BF16)
                inner = jnp.dot(s, vp[:, e * RET_V_DIM:(e + 1) * RET_V_DIM],
                                preferred_element_type=F32)
                qd = (qm.astype(F32) * qdec_ref[p]).astype(BF16)
                cross = jnp.dot(qd, st_b, preferred_element_type=F32)
                y = inner + cross
                mu = jnp.mean(y, axis=-1, keepdims=True)
                c = y - mu
                var = jnp.mean(c * c, axis=-1, keepdims=True)
                hc = slice(h * RET_V_DIM, (h + 1) * RET_V_DIM)
                yn = c * lax.rsqrt(var + EPS) * gng_ref[:, hc]
                oc = slice(SGU_WIDTH + h * RET_V_DIM, SGU_WIDTH + (h + 1) * RET_V_DIM)
                o_ref[rows, oc] = (gate_ref[rows, hc].astype(F32) * yn).astype(o_ref.dtype)
            kd = (kp.astype(F32) * kdec_ref[p]).astype(BF16)
            kv = lax.dot_general(kd, vp, (((0,), (0,)), ((), ())),
                                 preferred_element_type=F32)
            new = jnp.where(row_lo, kv[:, :RET_V_DIM], kv[:, RET_V_DIM:])
            state_ref[p] = st * cdec_ref[p] + new
        return carry

    lax.fori_loop(0, blocks_per_step, block, 0)


def _retention_tables():
    t = SGU_BLOCK
    gammas = 1.0 - jnp.exp2(-5.0 - jnp.arange(RET_HEADS, dtype=F32))
    log_g = jnp.log(gammas)
    idx = jnp.arange(t, dtype=F32)
    n = idx[:, None]
    m = idx[None, :]
    cn = jnp.floor(n / CHUNK)
    cm = jnp.floor(m / CHUNK)
    dist = jnp.where(cn == cm, jnp.abs(n - m), n - m)
    dec = jnp.exp(log_g[:, None, None] * dist[None])
    dmat = jnp.where((cn >= cm)[None], dec, 0.0)
    q_dec = jnp.exp(log_g[:, None] * (idx + 1.0)[None, :])
    k_dec = jnp.exp(log_g[:, None] * (t - 1.0 - idx)[None, :])
    c_dec = jnp.exp(log_g * t)

    def lanes_by_pair(per_head_rows):
        a = per_head_rows.reshape(RET_HEADS // 2, 2, t)
        return jnp.repeat(jnp.transpose(a, (0, 2, 1)), RET_QK_DIM, axis=2)

    qdec = lanes_by_pair(q_dec)
    kdec = lanes_by_pair(k_dec)
    cdec = jnp.repeat(c_dec.reshape(RET_HEADS // 2, 2), RET_QK_DIM, axis=1)
    cdec = jnp.broadcast_to(cdec[:, :, None], (RET_HEADS // 2, LANES, RET_V_DIM))
    return dmat, qdec, kdec, cdec


def _mixer(z, w_s, b_s, gn_g, seq, tb=512):
    m = z.shape[0]
    t = SGU_BLOCK
    dmat, qdec, kdec, cdec = _retention_tables()
    const3 = lambda s: (0, 0, 0)
    kern = functools.partial(_mixer_kernel, steps_per_seq=seq // tb,
                             blocks_per_step=tb // t)
    zspec = lambda c: pl.BlockSpec((tb, 1024), lambda s, c=c: (s, c))
    return pl.pallas_call(
        kern,
        grid=(m // tb,),
        in_specs=[zspec(0), zspec(1), zspec(2), zspec(3), zspec(4),
                  pl.BlockSpec((SGU_HEADS, t, t), const3),
                  pl.BlockSpec((t, SGU_HEADS), lambda s: (0, 0)),
                  pl.BlockSpec((RET_HEADS, t, t), const3),
                  pl.BlockSpec((RET_HEADS // 2, t, LANES), const3),
                  pl.BlockSpec((RET_HEADS // 2, t, LANES), const3),
                  pl.BlockSpec((RET_HEADS // 2, LANES, RET_V_DIM), const3),
                  pl.BlockSpec((1, RET_WIDTH), lambda s: (0, 0))],
        out_specs=pl.BlockSpec((tb, SGU_WIDTH + RET_WIDTH), lambda s: (s, 0)),
        out_shape=jax.ShapeDtypeStruct((m, SGU_WIDTH + RET_WIDTH), BF16),
        scratch_shapes=[pltpu.VMEM((RET_HEADS // 2, LANES, RET_V_DIM), F32)],
        compiler_params=_params(1),
        name="mixer",
    )(z, z, z, z, z, w_s.astype(BF16), jnp.transpose(b_s), dmat, qdec, kdec, cdec,
      gn_g.reshape(1, RET_WIDTH))


def _outproj_kernel(mix_ref, w_ref, x_ref, g_ref, x1_ref, h2_ref):
    x1 = x_ref[...] + jnp.dot(mix_ref[...], w_ref[...], preferred_element_type=F32)
    x1_ref[...] = x1
    h2_ref[...] = _rms(x1, g_ref[...]).astype(h2_ref.dtype)


def _out_proj(mix, w_out, x2d, g2, bm=512):
    m, k = mix.shape
    n = w_out.shape[1]
    return pl.pallas_call(
        _outproj_kernel,
        grid=(m // bm,),
        in_specs=[pl.BlockSpec((bm, k), lambda i: (i, 0)),
                  pl.BlockSpec((k, n), lambda i: (0, 0)),
                  pl.BlockSpec((bm, n), lambda i: (i, 0)),
                  pl.BlockSpec((1, n), lambda i: (0, 0))],
        out_specs=[pl.BlockSpec((bm, n), lambda i: (i, 0)),
                   pl.BlockSpec((bm, n), lambda i: (i, 0))],
        out_shape=[jax.ShapeDtypeStruct((m, n), F32),
                   jax.ShapeDtypeStruct((m, n), BF16)],
        compiler_params=_params(1),
        name="out_proj",
    )(mix, w_out, x2d, g2.reshape(1, n))


CARRY_ROWS = 8


def _upproj_kernel(h_ref, wg_ref, wv_ref, cwg_ref, cwv_ref, cbg_ref, cbv_ref,
                   o_ref, ug_ref, uv_ref, *, tiles_per_seq):
    i = pl.program_id(1)
    bm = h_ref.shape[0]

    @pl.when(i % tiles_per_seq == 0)
    def _():
        ug_ref[0:CARRY_ROWS, :] = jnp.zeros((CARRY_ROWS, ug_ref.shape[1]), F32)
        uv_ref[0:CARRY_ROWS, :] = jnp.zeros((CARRY_ROWS, uv_ref.shape[1]), F32)

    h = h_ref[...]
    ug_ref[CARRY_ROWS:, :] = jnp.dot(h, wg_ref[...], preferred_element_type=F32)
    uv_ref[CARRY_ROWS:, :] = jnp.dot(h, wv_ref[...], preferred_element_type=F32)

    def conv(u_ref, cw_ref, cb_ref):
        acc = u_ref[CARRY_ROWS - 2:CARRY_ROWS - 2 + bm, :] * cw_ref[0:1, :]
        acc = acc + u_ref[CARRY_ROWS - 1:CARRY_ROWS - 1 + bm, :] * cw_ref[1:2, :]
        acc = acc + u_ref[CARRY_ROWS:CARRY_ROWS + bm, :] * cw_ref[2:3, :]
        return acc + cb_ref[...]

    g = conv(ug_ref, cwg_ref, cbg_ref)
    v = conv(uv_ref, cwv_ref, cbv_ref)
    o_ref[...] = (jax.nn.silu(g) * v).astype(o_ref.dtype)

    ug_ref[0:CARRY_ROWS, :] = ug_ref[bm:bm + CARRY_ROWS, :]
    uv_ref[0:CARRY_ROWS, :] = uv_ref[bm:bm + CARRY_ROWS, :]


def _up_proj(h2, w_up, conv_w, conv_b, seq, bm=1024, bn=512):
    m, k = h2.shape
    f = w_up.shape[1] // 2
    nt = f // bn
    kern = functools.partial(_upproj_kernel, tiles_per_seq=seq // bm)
    cb = conv_b.reshape(1, 2 * f)
    return pl.pallas_call(
        kern,
        grid=(nt, m // bm),
        in_specs=[pl.BlockSpec((bm, k), lambda j, i: (i, 0)),
                  pl.BlockSpec((k, bn), lambda j, i: (0, j)),
                  pl.BlockSpec((k, bn), lambda j, i: (0, j + nt)),
                  pl.BlockSpec((3, bn), lambda j, i: (0, j)),
                  pl.BlockSpec((3, bn), lambda j, i: (0, j + nt)),
                  pl.BlockSpec((1, bn), lambda j, i: (0, j)),
                  pl.BlockSpec((1, bn), lambda j, i: (0, j + nt))],
        out_specs=pl.BlockSpec((bm, bn), lambda j, i: (i, j)),
        out_shape=jax.ShapeDtypeStruct((m, f), BF16),
        scratch_shapes=[pltpu.VMEM((bm + CARRY_ROWS, bn), F32),
                        pltpu.VMEM((bm + CARRY_ROWS, bn), F32)],
        compiler_params=_params(2),
        name="up_proj",
    )(h2, w_up, w_up, conv_w, conv_w, cb, cb)


def _downproj_kernel(a_ref, w_ref, x1_ref, g_ref, o_ref):
    kk = pl.program_id(1)
    part = jnp.dot(a_ref[...], w_ref[...], preferred_element_type=F32)

    @pl.when(kk == 0)
    def _():
        o_ref[...] = x1_ref[...] + part

    @pl.when(kk > 0)
    def _():
        o_ref[...] += part

    @pl.when(kk == pl.num_programs(1) - 1)
    def _():
        o_ref[...] = _rms(o_ref[...], g_ref[...])


def _down_proj(act, w_down, x1, g, bm=1024, bk=512):
    m, k = act.shape
    n = w_down.shape[1]
    return pl.pallas_call(
        _downproj_kernel,
        grid=(m // bm, k // bk),
        in_specs=[pl.BlockSpec((bm, bk), lambda i, kk: (i, kk)),
                  pl.BlockSpec((bk, n), lambda i, kk: (kk, 0)),
                  pl.BlockSpec((bm, n), lambda i, kk: (i, 0)),
                  pl.BlockSpec((1, n), lambda i, kk: (0, 0))],
        out_specs=pl.BlockSpec((bm, n), lambda i, kk: (i, 0)),
        out_shape=jax.ShapeDtypeStruct((m, n), F32),
        compiler_params=_params(2),
        name="down_proj",
    )(act, w_down, x1, g.reshape(1, n))


def _rotary_tables(seq):
    half = RET_QK_DIM // 2
    inv_freq = ROPE_BASE ** (-jnp.arange(0, RET_QK_DIM, 2, dtype=F32) / RET_QK_DIM)
    ang = jnp.arange(seq, dtype=F32)[:, None] * inv_freq[None, :]
    cos = jnp.cos(ang)
    sin = jnp.sin(ang)
    cos_t = jnp.tile(cos, (1, LANES // half))
    sin_t = jnp.tile(jnp.concatenate([-sin, sin], axis=1), (1, LANES // RET_QK_DIM))
    return cos_t, sin_t


def kernel(x, norm1_g, w_in, sgu_ln_g, sgu_ln_b, sgu_w_s, sgu_b_s, ret_gn_g,
           w_out, norm2_g, w_up, conv_w, conv_b, w_down, final_g):
    bsz, seq, d = x.shape
    assert w_in.shape[0] == 1, "single-layer block only"
    x2d = x.reshape(bsz * seq, d)
    cos_t, sin_t = _rotary_tables(seq)
    h1 = _norm(x2d, norm1_g[0])
    z = _in_proj(h1, w_in[0].astype(BF16), sgu_ln_g[0], sgu_ln_b[0], cos_t, sin_t, seq)
    mix = _mixer(z, sgu_w_s[0], sgu_b_s[0], ret_gn_g[0], seq)
    x1, h2 = _out_proj(mix, w_out[0].astype(BF16), x2d, norm2_g[0])
    act = _up_proj(h2, w_up[0].astype(BF16), conv_w[0], conv_b[0], seq)
    out = _down_proj(act, w_down[0].astype(BF16), x1, final_g)
    return out.reshape(bsz, seq, d)
```

```python
import functools

import jax
import jax.numpy as jnp
from jax import lax
from jax.experimental import pallas as pl
from jax.experimental.pallas import tpu as pltpu

F32 = jnp.float32
BF16 = jnp.bfloat16

D_MODEL = 2048
CHUNK = 64
SGU_WIDTH = 1024
SGU_HEADS = 8
SGU_BLOCK = 128
RET_WIDTH = 1024
RET_HEADS = 8
RET_V_DIM = 128
RET_QK_DIM = 64
RET_QK_WIDTH = 512
ROPE_BASE = 10000.0
D_FF = 5632
CONV_WIDTH = 3
EPS = 1e-6
LANES = 128
COL_TILE = 1024

VMEM_LIMIT_BYTES = 56 * 1024 * 1024
SUB_ROWS = 256
CAST_ROWS = 64


def _params(n_axes):
    return pltpu.CompilerParams(
        dimension_semantics=("arbitrary",) * n_axes,
        vmem_limit_bytes=VMEM_LIMIT_BYTES,
    )


def _resident(block_shape, index_map):
    return pl.BlockSpec(block_shape, index_map, pipeline_mode=pl.Buffered(1))


def _rms(x, g):
    ms = jnp.mean(x * x, axis=-1, keepdims=True)
    return x * lax.rsqrt(ms + EPS) * g


def _row_groups(n_rows):
    return [slice(r, r + SUB_ROWS) for r in range(0, n_rows, SUB_ROWS)]


def _inproj_u_kernel(x_ref, g_ref, w_ref, h1_ref, o_ref, wb_ref):
    @pl.when(pl.program_id(0) == 0)
    def _():
        wb_ref[...] = w_ref[...].astype(BF16)

    for rows in _row_groups(x_ref.shape[0]):
        h = _rms(x_ref[rows, :], g_ref[...]).astype(BF16)
        h1_ref[rows, :] = h
        acc = jnp.dot(h, wb_ref[...], preferred_element_type=F32)
        o_ref[rows, :] = jax.nn.gelu(acc).astype(o_ref.dtype)


def _in_proj_u(x2d, g1, w_in, bm=1024):
    m, d = x2d.shape
    return pl.pallas_call(
        _inproj_u_kernel,
        grid=(m // bm,),
        in_specs=[pl.BlockSpec((bm, d), lambda i: (i, 0)),
                  pl.BlockSpec((1, d), lambda i: (0, 0)),
                  _resident((d, COL_TILE), lambda i: (0, 0))],
        out_specs=[pl.BlockSpec((bm, d), lambda i: (i, 0)),
                   pl.BlockSpec((bm, COL_TILE), lambda i: (i, 0))],
        out_shape=[jax.ShapeDtypeStruct((m, d), BF16),
                   jax.ShapeDtypeStruct((m, COL_TILE), BF16)],
        scratch_shapes=[pltpu.VMEM((d, COL_TILE), BF16)],
        compiler_params=_params(1),
        name="in_proj_u",
    )(x2d, g1.reshape(1, d), w_in)


def _inproj_kernel(h_ref, w_ref, lng_ref, lnb_ref, cos_ref, sin_ref, wo_ref,
                   o_ref, wob_ref, wb_ref):
    j = pl.program_id(0)

    @pl.when(pl.program_id(1) == 0)
    def _():
        wb_ref[...] = w_ref[...].astype(BF16)

    wob_ref[...] = wo_ref[...].astype(BF16)

    def ln_gelu(acc, rows):
        a = jax.nn.gelu(acc)
        mu = jnp.mean(a, axis=-1, keepdims=True)
        c = a - mu
        var = jnp.mean(c * c, axis=-1, keepdims=True)
        return c * lax.rsqrt(var + EPS) * lng_ref[...] + lnb_ref[...]

    def rotary(acc, rows):
        cos = cos_ref[rows, :]
        sin = sin_ref[rows, :]
        lane = lax.broadcasted_iota(jnp.int32, cos.shape, 1)
        first_half = (lane % RET_QK_DIM) < (RET_QK_DIM // 2)
        n_col = COL_TILE // LANES
        out = []
        for c in range(n_col):
            xc = acc[:, c * LANES:(c + 1) * LANES]
            partner = jnp.where(first_half,
                                pltpu.roll(xc, LANES - RET_QK_DIM // 2, 1),
                                pltpu.roll(xc, RET_QK_DIM // 2, 1))
            r = xc * cos + partner * sin
            if c >= n_col // 2:
                r = r * (RET_QK_DIM ** -0.5)
            out.append(r)
        return jnp.concatenate(out, axis=1)

    epilogues = (ln_gelu, rotary, lambda acc, rows: acc,
                 lambda acc, rows: jax.nn.silu(acc))

    for t, epilogue in enumerate(epilogues):
        @pl.when(j == t)
        def _(epilogue=epilogue):
            for rows in _row_groups(h_ref.shape[0]):
                acc = jnp.dot(h_ref[rows, :], wb_ref[...], preferred_element_type=F32)
                o_ref[rows, :] = epilogue(acc, rows).astype(o_ref.dtype)


def _in_proj(h1, w_in, ln_g, ln_b, cos_t, sin_t, w_out, seq, bm=1024):
    m, k = h1.shape
    nt = w_in.shape[1] // COL_TILE - 1
    mt = m // bm
    seq_tiles = seq // bm
    assert nt * mt * CAST_ROWS == w_out.shape[0]
    slab = lambda j, i: (j * mt + i, 0)
    return pl.pallas_call(
        _inproj_kernel,
        grid=(nt, mt),
        in_specs=[pl.BlockSpec((bm, k), lambda j, i: (i, 0)),
                  pl.BlockSpec((k, COL_TILE), lambda j, i: (0, j + 1)),
                  pl.BlockSpec((1, COL_TILE), lambda j, i: (0, 0)),
                  pl.BlockSpec((1, COL_TILE), lambda j, i: (0, 0)),
                  pl.BlockSpec((bm, LANES), lambda j, i: (i % seq_tiles, 0)),
                  pl.BlockSpec((bm, LANES), lambda j, i: (i % seq_tiles, 0)),
                  pl.BlockSpec((CAST_ROWS, w_out.shape[1]), slab)],
        out_specs=[pl.BlockSpec((bm, COL_TILE), lambda j, i: (i, j)),
                   pl.BlockSpec((CAST_ROWS, w_out.shape[1]), slab)],
        out_shape=[jax.ShapeDtypeStruct((m, nt * COL_TILE), BF16),
                   jax.ShapeDtypeStruct(w_out.shape, BF16)],
        scratch_shapes=[pltpu.VMEM((k, COL_TILE), BF16)],
        compiler_params=_params(2),
        name="in_proj",
    )(h1, w_in, ln_g.reshape(1, COL_TILE), ln_b.reshape(1, COL_TILE), cos_t, sin_t, w_out)


def _mixer_kernel(gu_ref, vn_ref, qk_ref, vr_ref, gate_ref, ws_ref, bs_ref,
                  dmat_ref, qdec_ref, kdec_ref, cdec_ref, gng_ref,
                  o_ref, state_ref, *, steps_per_seq, blocks_per_step):
    step = pl.program_id(0)

    @pl.when(step % steps_per_seq == 0)
    def _():
        state_ref[...] = jnp.zeros_like(state_ref)

    t = SGU_BLOCK
    row_i = lax.broadcasted_iota(jnp.int32, (t, t), 0)
    col_i = lax.broadcasted_iota(jnp.int32, (t, t), 1)
    sgu_mask = (row_i // CHUNK) >= (col_i // CHUNK)
    lane_lo = col_i < RET_QK_DIM
    row_lo = row_i < RET_QK_DIM

    def block(r, carry):
        rows = pl.ds(pl.multiple_of(r * t, t), t)

        for h in range(SGU_HEADS):
            cols = slice(h * t, (h + 1) * t)
            w = jnp.where(sgu_mask, ws_ref[h], jnp.zeros((), ws_ref.dtype))
            mixed = jnp.dot(w, vn_ref[rows, cols], preferred_element_type=F32)
            mixed = mixed + bs_ref[h]
            o_ref[rows, cols] = (gu_ref[rows, cols].astype(F32) * mixed).astype(o_ref.dtype)

        for p in range(RET_HEADS // 2):
            qp = qk_ref[rows, p * LANES:(p + 1) * LANES]
            kp = qk_ref[rows, RET_QK_WIDTH + p * LANES:RET_QK_WIDTH + (p + 1) * LANES]
            vp = vr_ref[rows, 2 * p * RET_V_DIM:2 * (p + 1) * RET_V_DIM]
            st = state_ref[p]
            st_b = st.astype(BF16)
            for e in range(2):
                h = 2 * p + e
                keep = lane_lo if e == 0 else jnp.logical_not(lane_lo)
                qm = jnp.where(keep, qp, jnp.zeros((), qp.dtype))
                s = lax.dot_general(qm, kp, (((1,), (1,)), ((), ())),
                                    preferred_element_type=F32)
                s = (s * dmat_ref[h]).astype(BF16)
                inner = jnp.dot(s, vp[:, e * RET_V_DIM:(e + 1) * RET_V_DIM],
                                preferred_element_type=F32)
                qd = (qm.astype(F32) * qdec_ref[p]).astype(BF16)
                cross = jnp.dot(qd, st_b, preferred_element_type=F32)
                y = inner + cross
                mu = jnp.mean(y, axis=-1, keepdims=True)
                c = y - mu
                var = jnp.mean(c * c, axis=-1, keepdims=True)
                hc = slice(h * RET_V_DIM, (h + 1) * RET_V_DIM)
                yn = c * lax.rsqrt(var + EPS) * gng_ref[:, hc]
                oc = slice(SGU_WIDTH + h * RET_V_DIM, SGU_WIDTH + (h + 1) * RET_V_DIM)
                o_ref[rows, oc] = (gate_ref[rows, hc].astype(F32) * yn).astype(o_ref.dtype)
            kd = (kp.astype(F32) * kdec_ref[p]).astype(BF16)
            kv = lax.dot_general(kd, vp, (((0,), (0,)), ((), ())),
                                 preferred_element_type=F32)
            new = jnp.where(row_lo, kv[:, :RET_V_DIM], kv[:, RET_V_DIM:])
            state_ref[p] = st * cdec_ref[p] + new
        return carry

    lax.fori_loop(0, blocks_per_step, block, 0)


def _retention_tables():
    t = SGU_BLOCK
    gammas = 1.0 - jnp.exp2(-5.0 - jnp.arange(RET_HEADS, dtype=F32))
    log_g = jnp.log(gammas)
    idx = jnp.arange(t, dtype=F32)
    n = idx[:, None]
    m = idx[None, :]
    cn = jnp.floor(n / CHUNK)
    cm = jnp.floor(m / CHUNK)
    dist = jnp.where(cn == cm, jnp.abs(n - m), n - m)
    dec = jnp.exp(log_g[:, None, None] * dist[None])
    dmat = jnp.where((cn >= cm)[None], dec, 0.0)
    q_dec = jnp.exp(log_g[:, None] * (idx + 1.0)[None, :])
    k_dec = jnp.exp(log_g[:, None] * (t - 1.0 - idx)[None, :])
    c_dec = jnp.exp(log_g * t)

    def lanes_by_pair(per_head_rows):
        a = per_head_rows.reshape(RET_HEADS // 2, 2, t)
        return jnp.repeat(jnp.transpose(a, (0, 2, 1)), RET_QK_DIM, axis=2)

    qdec = lanes_by_pair(q_dec)
    kdec = lanes_by_pair(k_dec)
    cdec = jnp.repeat(c_dec.reshape(RET_HEADS // 2, 2), RET_QK_DIM, axis=1)
    cdec = jnp.broadcast_to(cdec[:, :, None], (RET_HEADS // 2, LANES, RET_V_DIM))
    return dmat, qdec, kdec, cdec


def _mixer(zu, z, w_s, b_s, gn_g, seq, tb=512):
    m = z.shape[0]
    t = SGU_BLOCK
    dmat, qdec, kdec, cdec = _retention_tables()
    bias = jnp.broadcast_to(b_s[:, :, None], (SGU_HEADS, t, t))
    const3 = lambda s: (0, 0, 0)
    kern = functools.partial(_mixer_kernel, steps_per_seq=seq // tb,
                             blocks_per_step=tb // t)
    zspec = lambda c: pl.BlockSpec((tb, COL_TILE), lambda s, c=c: (s, c))
    return pl.pallas_call(
        kern,
        grid=(m // tb,),
        in_specs=[zspec(0), zspec(0), zspec(1), zspec(2), zspec(3),
                  pl.BlockSpec((SGU_HEADS, t, t), const3),
                  pl.BlockSpec((SGU_HEADS, t, t), const3),
                  pl.BlockSpec((RET_HEADS, t, t), const3),
                  pl.BlockSpec((RET_HEADS // 2, t, LANES), const3),
                  pl.BlockSpec((RET_HEADS // 2, t, LANES), const3),
                  pl.BlockSpec((RET_HEADS // 2, LANES, RET_V_DIM), const3),
                  pl.BlockSpec((1, RET_WIDTH), lambda s: (0, 0))],
        out_specs=pl.BlockSpec((tb, SGU_WIDTH + RET_WIDTH), lambda s: (s, 0)),
        out_shape=jax.ShapeDtypeStruct((m, SGU_WIDTH + RET_WIDTH), BF16),
        scratch_shapes=[pltpu.VMEM((RET_HEADS // 2, LANES, RET_V_DIM), F32)],
        compiler_params=_params(1),
        name="mixer",
    )(zu, z, z, z, z, w_s.astype(BF16), bias, dmat, qdec, kdec, cdec,
      gn_g.reshape(1, RET_WIDTH))


def _outproj_kernel(mix_ref, w_ref, x_ref, g_ref, x1_ref, h2_ref):
    for rows in _row_groups(mix_ref.shape[0]):
        x1 = x_ref[rows, :] + jnp.dot(mix_ref[rows, :], w_ref[...],
                                      preferred_element_type=F32)
        x1_ref[rows, :] = x1
        h2_ref[rows, :] = _rms(x1, g_ref[...]).astype(h2_ref.dtype)


def _out_proj(mix, w_out_b, x2d, g2, bm=512):
    m, k = mix.shape
    n = w_out_b.shape[1]
    return pl.pallas_call(
        _outproj_kernel,
        grid=(m // bm,),
        in_specs=[pl.BlockSpec((bm, k), lambda i: (i, 0)),
                  _resident((k, n), lambda i: (0, 0)),
                  pl.BlockSpec((bm, n), lambda i: (i, 0)),
                  pl.BlockSpec((1, n), lambda i: (0, 0))],
        out_specs=[pl.BlockSpec((bm, n), lambda i: (i, 0)),
                   pl.BlockSpec((bm, n), lambda i: (i, 0))],
        out_shape=[jax.ShapeDtypeStruct((m, n), F32),
                   jax.ShapeDtypeStruct((m, n), BF16)],
        compiler_params=_params(1),
        name="out_proj",
    )(mix, w_out_b, x2d, g2.reshape(1, n))


CARRY_ROWS = 8


def _upproj_kernel(h_ref, wg_ref, wv_ref, cwg_ref, cwv_ref, cbg_ref, cbv_ref, wd_ref,
                   o_ref, wdb_ref, wb_ref, tail_ref, *, tiles_per_seq):
    i = pl.program_id(1)
    bn = wg_ref.shape[1]

    @pl.when(i == 0)
    def _():
        wb_ref[:, :bn] = wg_ref[...].astype(BF16)
        wb_ref[:, bn:] = wv_ref[...].astype(BF16)

    @pl.when(i % tiles_per_seq == 0)
    def _():
        tail_ref[...] = jnp.zeros_like(tail_ref)

    wdb_ref[...] = wd_ref[...].astype(BF16)

    cw = jnp.concatenate([cwg_ref[...], cwv_ref[...]], axis=1)
    cb = jnp.concatenate([cbg_ref[...], cbv_ref[...]], axis=1)

    tail = tail_ref[...]
    for rows in _row_groups(h_ref.shape[0]):
        up = jnp.dot(h_ref[rows, :], wb_ref[...], preferred_element_type=F32)
        ext = jnp.concatenate([tail, up], axis=0)
        conv = cb + up * cw[CONV_WIDTH - 1:CONV_WIDTH, :]
        for back in range(1, CONV_WIDTH):
            shifted = pltpu.roll(ext, back, 0)[CARRY_ROWS:, :]
            conv = conv + shifted * cw[CONV_WIDTH - 1 - back:CONV_WIDTH - back, :]
        o_ref[rows, :] = (jax.nn.silu(conv[:, :bn]) * conv[:, bn:]).astype(o_ref.dtype)
        tail = up[-CARRY_ROWS:, :]
    tail_ref[...] = tail


def _up_proj(h2, w_up, conv_w, conv_b, w_down, seq, bm=1024, bn=512):
    m, k = h2.shape
    f = w_up.shape[1] // 2
    nt = f // bn
    mt = m // bm
    assert nt * mt * CAST_ROWS == w_down.shape[0]
    kern = functools.partial(_upproj_kernel, tiles_per_seq=seq // bm)
    cb = conv_b.reshape(1, 2 * f)
    slab = lambda j, i: (j * mt + i, 0)
    return pl.pallas_call(
        kern,
        grid=(nt, mt),
        in_specs=[pl.BlockSpec((bm, k), lambda j, i: (i, 0)),
                  pl.BlockSpec((k, bn), lambda j, i: (0, j)),
                  pl.BlockSpec((k, bn), lambda j, i: (0, j + nt)),
                  pl.BlockSpec((CONV_WIDTH, bn), lambda j, i: (0, j)),
                  pl.BlockSpec((CONV_WIDTH, bn), lambda j, i: (0, j + nt)),
                  pl.BlockSpec((1, bn), lambda j, i: (0, j)),
                  pl.BlockSpec((1, bn), lambda j, i: (0, j + nt)),
                  pl.BlockSpec((CAST_ROWS, w_down.shape[1]), slab)],
        out_specs=[pl.BlockSpec((bm, bn), lambda j, i: (i, j)),
                   pl.BlockSpec((CAST_ROWS, w_down.shape[1]), slab)],
        out_shape=[jax.ShapeDtypeStruct((m, f), BF16),
                   jax.ShapeDtypeStruct(w_down.shape, BF16)],
        scratch_shapes=[pltpu.VMEM((k, 2 * bn), BF16),
                        pltpu.VMEM((CARRY_ROWS, 2 * bn), F32)],
        compiler_params=_params(2),
        name="up_proj",
    )(h2, w_up, w_up, conv_w, conv_w, cb, cb, w_down)


def _downproj_kernel(a_ref, w_ref, x1_ref, g_ref, o_ref):
    for rows in _row_groups(a_ref.shape[0]):
        x2 = x1_ref[rows, :] + jnp.dot(a_ref[rows, :], w_ref[...],
                                       preferred_element_type=F32)
        o_ref[rows, :] = _rms(x2, g_ref[...])


def _down_proj(act, w_down_b, x1, g, bm=512):
    m, k = act.shape
    n = w_down_b.shape[1]
    return pl.pallas_call(
        _downproj_kernel,
        grid=(m // bm,),
        in_specs=[pl.BlockSpec((bm, k), lambda i: (i, 0)),
                  _resident((k, n), lambda i: (0, 0)),
                  pl.BlockSpec((bm, n), lambda i: (i, 0)),
                  pl.BlockSpec((1, n), lambda i: (0, 0))],
        out_specs=pl.BlockSpec((bm, n), lambda i: (i, 0)),
        out_shape=jax.ShapeDtypeStruct((m, n), F32),
        compiler_params=_params(1),
        name="down_proj",
    )(act, w_down_b, x1, g.reshape(1, n))


def _rotary_tables(seq):
    half = RET_QK_DIM // 2
    inv_freq = ROPE_BASE ** (-jnp.arange(0, RET_QK_DIM, 2, dtype=F32) / RET_QK_DIM)
    ang = jnp.arange(seq, dtype=F32)[:, None] * inv_freq[None, :]
    cos = jnp.cos(ang)
    sin = jnp.sin(ang)
    cos_t = jnp.tile(cos, (1, LANES // half))
    sin_t = jnp.tile(jnp.concatenate([-sin, sin], axis=1), (1, LANES // RET_QK_DIM))
    return cos_t, sin_t


def kernel(x, norm1_g, w_in, sgu_ln_g, sgu_ln_b, sgu_w_s, sgu_b_s, ret_gn_g,
           w_out, norm2_g, w_up, conv_w, conv_b, w_down, final_g):
    bsz, seq, d = x.shape
    assert w_in.shape[0] == 1, "single-layer block only"
    x2d = x.reshape(bsz * seq, d)
    cos_t, sin_t = _rotary_tables(seq)
    h1, zu = _in_proj_u(x2d, norm1_g[0], w_in[0])
    z, w_out_b = _in_proj(h1, w_in[0], sgu_ln_g[0], sgu_ln_b[0], cos_t, sin_t, w_out[0], seq)
    mix = _mixer(zu, z, sgu_w_s[0], sgu_b_s[0], ret_gn_g[0], seq)
    x1, h2 = _out_proj(mix, w_out_b, x2d, norm2_g[0])
    act, w_down_b = _up_proj(h2, w_up[0], conv_w[0], conv_b[0], w_down[0], seq)
    out = _down_proj(act, w_down_b, x1, final_g)
    return out.reshape(bsz, seq, d)
```

```python
import functools

import jax
import jax.numpy as jnp
from jax import lax
from jax.experimental import pallas as pl
from jax.experimental.pallas import tpu as pltpu

F32 = jnp.float32
BF16 = jnp.bfloat16

D_MODEL = 2048
CHUNK = 64
SGU_WIDTH = 1024
SGU_HEADS = 8
SGU_BLOCK = 128
RET_WIDTH = 1024
RET_HEADS = 8
RET_V_DIM = 128
RET_QK_DIM = 64
RET_QK_WIDTH = 512
ROPE_BASE = 10000.0
D_FF = 5632
CONV_WIDTH = 3
EPS = 1e-6
LANES = 128
COL_TILE = 1024

VMEM_LIMIT_BYTES = 56 * 1024 * 1024
SUB_ROWS = 256
CAST_ROWS = 64


def _params(n_axes):
    return pltpu.CompilerParams(
        dimension_semantics=("arbitrary",) * n_axes,
        vmem_limit_bytes=VMEM_LIMIT_BYTES,
    )


def _resident(block_shape, index_map):
    return pl.BlockSpec(block_shape, index_map, pipeline_mode=pl.Buffered(1))


def _rms(x, g):
    ms = jnp.mean(x * x, axis=-1, keepdims=True)
    return x * lax.rsqrt(ms + EPS) * g


def _row_groups(n_rows):
    return [slice(r, r + SUB_ROWS) for r in range(0, n_rows, SUB_ROWS)]


def _inproj_u_kernel(x_ref, g_ref, w_ref, h1_ref, o_ref, wb_ref):
    @pl.when(pl.program_id(0) == 0)
    def _():
        wb_ref[...] = w_ref[...].astype(BF16)

    for rows in _row_groups(x_ref.shape[0]):
        h = _rms(x_ref[rows, :], g_ref[...]).astype(BF16)
        h1_ref[rows, :] = h
        acc = jnp.dot(h, wb_ref[...], preferred_element_type=F32)
        o_ref[rows, :] = jax.nn.gelu(acc).astype(o_ref.dtype)


def _in_proj_u(x2d, g1, w_in, bm=1024):
    m, d = x2d.shape
    return pl.pallas_call(
        _inproj_u_kernel,
        grid=(m // bm,),
        in_specs=[pl.BlockSpec((bm, d), lambda i: (i, 0)),
                  pl.BlockSpec((1, d), lambda i: (0, 0)),
                  _resident((d, COL_TILE), lambda i: (0, 0))],
        out_specs=[pl.BlockSpec((bm, d), lambda i: (i, 0)),
                   pl.BlockSpec((bm, COL_TILE), lambda i: (i, 0))],
        out_shape=[jax.ShapeDtypeStruct((m, d), BF16),
                   jax.ShapeDtypeStruct((m, COL_TILE), BF16)],
        scratch_shapes=[pltpu.VMEM((d, COL_TILE), BF16)],
        compiler_params=_params(1),
        name="in_proj_u",
    )(x2d, g1.reshape(1, d), w_in)


def _inproj_kernel(h_ref, w_ref, lng_ref, lnb_ref, cos_ref, sin_ref, wo_ref,
                   o_ref, wob_ref, wb_ref):
    j = pl.program_id(0)

    @pl.when(pl.program_id(1) == 0)
    def _():
        wb_ref[...] = w_ref[...].astype(BF16)

    wob_ref[...] = wo_ref[...].astype(BF16)

    def ln_gelu(acc, rows):
        a = jax.nn.gelu(acc)
        mu = jnp.mean(a, axis=-1, keepdims=True)
        c = a - mu
        var = jnp.mean(c * c, axis=-1, keepdims=True)
        return c * lax.rsqrt(var + EPS) * lng_ref[...] + lnb_ref[...]

    def rotary(acc, rows):
        cos = cos_ref[rows, :]
        sin = sin_ref[rows, :]
        lane = lax.broadcasted_iota(jnp.int32, cos.shape, 1)
        first_half = (lane % RET_QK_DIM) < (RET_QK_DIM // 2)
        n_col = COL_TILE // LANES
        out = []
        for c in range(n_col):
            xc = acc[:, c * LANES:(c + 1) * LANES]
            partner = jnp.where(first_half,
                                pltpu.roll(xc, LANES - RET_QK_DIM // 2, 1),
                                pltpu.roll(xc, RET_QK_DIM // 2, 1))
            r = xc * cos + partner * sin
            if c >= n_col // 2:
                r = r * (RET_QK_DIM ** -0.5)
            out.append(r)
        return jnp.concatenate(out, axis=1)

    epilogues = (ln_gelu, rotary, lambda acc, rows: acc,
                 lambda acc, rows: jax.nn.silu(acc))

    for t, epilogue in enumerate(epilogues):
        @pl.when(j == t)
        def _(epilogue=epilogue):
            for rows in _row_groups(h_ref.shape[0]):
                acc = jnp.dot(h_ref[rows, :], wb_ref[...], preferred_element_type=F32)
                o_ref[rows, :] = epilogue(acc, rows).astype(o_ref.dtype)


def _in_proj(h1, w_in, ln_g, ln_b, cos_t, sin_t, w_out, seq, bm=1024):
    m, k = h1.shape
    nt = w_in.shape[1] // COL_TILE - 1
    mt = m // bm
    seq_tiles = seq // bm
    assert nt * mt * CAST_ROWS == w_out.shape[0]
    slab = lambda j, i: (j * mt + i, 0)
    return pl.pallas_call(
        _inproj_kernel,
        grid=(nt, mt),
        in_specs=[pl.BlockSpec((bm, k), lambda j, i: (i, 0)),
                  pl.BlockSpec((k, COL_TILE), lambda j, i: (0, j + 1)),
                  pl.BlockSpec((1, COL_TILE), lambda j, i: (0, 0)),
                  pl.BlockSpec((1, COL_TILE), lambda j, i: (0, 0)),
                  pl.BlockSpec((bm, LANES), lambda j, i: (i % seq_tiles, 0)),
                  pl.BlockSpec((bm, LANES), lambda j, i: (i % seq_tiles, 0)),
                  pl.BlockSpec((CAST_ROWS, w_out.shape[1]), slab)],
        out_specs=[pl.BlockSpec((bm, COL_TILE), lambda j, i: (i, j)),
                   pl.BlockSpec((CAST_ROWS, w_out.shape[1]), slab)],
        out_shape=[jax.ShapeDtypeStruct((m, nt * COL_TILE), BF16),
                   jax.ShapeDtypeStruct(w_out.shape, BF16)],
        scratch_shapes=[pltpu.VMEM((k, COL_TILE), BF16)],
        compiler_params=_params(2),
        name="in_proj",
    )(h1, w_in, ln_g.reshape(1, COL_TILE), ln_b.reshape(1, COL_TILE), cos_t, sin_t, w_out)


RET_BLOCK = 256


def _mixer_kernel(gu_ref, vn_ref, qk_ref, vr_ref, gate_ref, ws_ref, bs_ref,
                  dmat_ref, qdec_ref, kdec_ref, cdec_ref, gng_ref,
                  o_ref, state_ref, *, steps_per_seq, blocks_per_step):
    step = pl.program_id(0)

    @pl.when(step % steps_per_seq == 0)
    def _():
        state_ref[...] = jnp.zeros_like(state_ref)

    t = SGU_BLOCK
    rb = RET_BLOCK
    row_i = lax.broadcasted_iota(jnp.int32, (t, t), 0)
    col_i = lax.broadcasted_iota(jnp.int32, (t, t), 1)
    sgu_mask = (row_i // CHUNK) >= (col_i // CHUNK)
    lane_lo = lax.broadcasted_iota(jnp.int32, (rb, LANES), 1) < RET_QK_DIM
    st_row = lax.broadcasted_iota(jnp.int32, (LANES, 2 * RET_V_DIM), 0) < RET_QK_DIM
    st_col = lax.broadcasted_iota(jnp.int32, (LANES, 2 * RET_V_DIM), 1) < RET_V_DIM
    own_block = st_row == st_col

    for b in range(blocks_per_step):
        r0 = b * rb
        rows = slice(r0, r0 + rb)
        ra = slice(r0, r0 + t)
        rc = slice(r0 + t, r0 + 2 * t)

        for h in range(SGU_HEADS):
            cols = slice(h * t, (h + 1) * t)
            w = jnp.where(sgu_mask, ws_ref[h], jnp.zeros((), ws_ref.dtype))
            vb = jnp.concatenate([vn_ref[ra, cols], vn_ref[rc, cols]], axis=1)
            mixed = jnp.dot(w, vb, preferred_element_type=F32) + bs_ref[h]
            o_ref[ra, cols] = (gu_ref[ra, cols].astype(F32) * mixed[:, :t]).astype(o_ref.dtype)
            o_ref[rc, cols] = (gu_ref[rc, cols].astype(F32) * mixed[:, t:]).astype(o_ref.dtype)

        for p in range(RET_HEADS // 2):
            qp = qk_ref[rows, p * LANES:(p + 1) * LANES]
            kp = qk_ref[rows, RET_QK_WIDTH + p * LANES:RET_QK_WIDTH + (p + 1) * LANES]
            vp = vr_ref[rows, 2 * p * RET_V_DIM:2 * (p + 1) * RET_V_DIM]
            st = state_ref[p]
            qd = (qp.astype(F32) * qdec_ref[p]).astype(BF16)
            cross = jnp.dot(qd, st.astype(BF16), preferred_element_type=F32)
            for e in range(2):
                h = 2 * p + e
                keep = lane_lo if e == 0 else jnp.logical_not(lane_lo)
                qm = jnp.where(keep, qp, jnp.zeros((), qp.dtype))
                s = lax.dot_general(qm, kp, (((1,), (1,)), ((), ())),
                                    preferred_element_type=F32)
                s = (s * dmat_ref[h]).astype(BF16)
                hv = slice(e * RET_V_DIM, (e + 1) * RET_V_DIM)
                y = jnp.dot(s, vp[:, hv], preferred_element_type=F32) + cross[:, hv]
                mu = jnp.mean(y, axis=-1, keepdims=True)
                c = y - mu
                var = jnp.mean(c * c, axis=-1, keepdims=True)
                hc = slice(h * RET_V_DIM, (h + 1) * RET_V_DIM)
                yn = c * lax.rsqrt(var + EPS) * gng_ref[:, hc]
                oc = slice(SGU_WIDTH + h * RET_V_DIM, SGU_WIDTH + (h + 1) * RET_V_DIM)
                o_ref[rows, oc] = (gate_ref[rows, hc].astype(F32) * yn).astype(o_ref.dtype)
            kd = (kp.astype(F32) * kdec_ref[p]).astype(BF16)
            kv = lax.dot_general(kd, vp, (((0,), (0,)), ((), ())),
                                 preferred_element_type=F32)
            state_ref[p] = st * cdec_ref[p] + jnp.where(own_block, kv, 0.0)


def _retention_tables():
    t = RET_BLOCK
    gammas = 1.0 - jnp.exp2(-5.0 - jnp.arange(RET_HEADS, dtype=F32))
    log_g = jnp.log(gammas)
    idx = jnp.arange(t, dtype=F32)
    n = idx[:, None]
    m = idx[None, :]
    cn = jnp.floor(n / CHUNK)
    cm = jnp.floor(m / CHUNK)
    dist = jnp.where(cn == cm, jnp.abs(n - m), n - m)
    dec = jnp.exp(log_g[:, None, None] * dist[None])
    dmat = jnp.where((cn >= cm)[None], dec, 0.0)
    q_dec = jnp.exp(log_g[:, None] * (idx + 1.0)[None, :])
    k_dec = jnp.exp(log_g[:, None] * (t - 1.0 - idx)[None, :])
    c_dec = jnp.exp(log_g * t)

    def lanes_by_pair(per_head_rows):
        a = per_head_rows.reshape(RET_HEADS // 2, 2, t)
        return jnp.repeat(jnp.transpose(a, (0, 2, 1)), RET_QK_DIM, axis=2)

    qdec = lanes_by_pair(q_dec)
    kdec = lanes_by_pair(k_dec)
    cdec = jnp.repeat(c_dec.reshape(RET_HEADS // 2, 2), RET_QK_DIM, axis=1)
    cdec = jnp.broadcast_to(cdec[:, :, None], (RET_HEADS // 2, LANES, 2 * RET_V_DIM))
    return dmat, qdec, kdec, cdec


def _mixer(zu, z, w_s, b_s, gn_g, seq, tb=512):
    m = z.shape[0]
    t = SGU_BLOCK
    rb = RET_BLOCK
    dmat, qdec, kdec, cdec = _retention_tables()
    bias = jnp.broadcast_to(b_s[:, :, None], (SGU_HEADS, t, 2 * t))
    const3 = lambda s: (0, 0, 0)
    kern = functools.partial(_mixer_kernel, steps_per_seq=seq // tb,
                             blocks_per_step=tb // rb)
    zspec = lambda c: pl.BlockSpec((tb, COL_TILE), lambda s, c=c: (s, c))
    return pl.pallas_call(
        kern,
        grid=(m // tb,),
        in_specs=[zspec(0), zspec(0), zspec(1), zspec(2), zspec(3),
                  pl.BlockSpec((SGU_HEADS, t, t), const3),
                  pl.BlockSpec((SGU_HEADS, t, 2 * t), const3),
                  pl.BlockSpec((RET_HEADS, rb, rb), const3),
                  pl.BlockSpec((RET_HEADS // 2, rb, LANES), const3),
                  pl.BlockSpec((RET_HEADS // 2, rb, LANES), const3),
                  pl.BlockSpec((RET_HEADS // 2, LANES, 2 * RET_V_DIM), const3),
                  pl.BlockSpec((1, RET_WIDTH), lambda s: (0, 0))],
        out_specs=pl.BlockSpec((tb, SGU_WIDTH + RET_WIDTH), lambda s: (s, 0)),
        out_shape=jax.ShapeDtypeStruct((m, SGU_WIDTH + RET_WIDTH), BF16),
        scratch_shapes=[pltpu.VMEM((RET_HEADS // 2, LANES, 2 * RET_V_DIM), F32)],
        compiler_params=_params(1),
        name="mixer",
    )(zu, z, z, z, z, w_s.astype(BF16), bias, dmat, qdec, kdec, cdec,
      gn_g.reshape(1, RET_WIDTH))


def _outproj_kernel(mix_ref, w_ref, x_ref, g_ref, x1_ref, h2_ref):
    for rows in _row_groups(mix_ref.shape[0]):
        x1 = x_ref[rows, :] + jnp.dot(mix_ref[rows, :], w_ref[...],
                                      preferred_element_type=F32)
        x1_ref[rows, :] = x1
        h2_ref[rows, :] = _rms(x1, g_ref[...]).astype(h2_ref.dtype)


def _out_proj(mix, w_out_b, x2d, g2, bm=512):
    m, k = mix.shape
    n = w_out_b.shape[1]
    return pl.pallas_call(
        _outproj_kernel,
        grid=(m // bm,),
        in_specs=[pl.BlockSpec((bm, k), lambda i: (i, 0)),
                  _resident((k, n), lambda i: (0, 0)),
                  pl.BlockSpec((bm, n), lambda i: (i, 0)),
                  pl.BlockSpec((1, n), lambda i: (0, 0))],
        out_specs=[pl.BlockSpec((bm, n), lambda i: (i, 0)),
                   pl.BlockSpec((bm, n), lambda i: (i, 0))],
        out_shape=[jax.ShapeDtypeStruct((m, n), F32),
                   jax.ShapeDtypeStruct((m, n), BF16)],
        compiler_params=_params(1),
        name="out_proj",
    )(mix, w_out_b, x2d, g2.reshape(1, n))


CARRY_ROWS = 8


def _upproj_kernel(h_ref, wg_ref, wv_ref, cwg_ref, cwv_ref, cbg_ref, cbv_ref, wd_ref,
                   o_ref, wdb_ref, wgb_ref, wvb_ref, ug_ref, uv_ref, *, tiles_per_seq):
    i = pl.program_id(1)
    bm = h_ref.shape[0]

    @pl.when(i == 0)
    def _():
        wgb_ref[...] = wg_ref[...].astype(BF16)
        wvb_ref[...] = wv_ref[...].astype(BF16)

    @pl.when(i % tiles_per_seq == 0)
    def _():
        ug_ref[0:CARRY_ROWS, :] = jnp.zeros((CARRY_ROWS, ug_ref.shape[1]), F32)
        uv_ref[0:CARRY_ROWS, :] = jnp.zeros((CARRY_ROWS, uv_ref.shape[1]), F32)

    wdb_ref[...] = wd_ref[...].astype(BF16)

    h = h_ref[...]
    ug_ref[CARRY_ROWS:, :] = jnp.dot(h, wgb_ref[...], preferred_element_type=F32)
    uv_ref[CARRY_ROWS:, :] = jnp.dot(h, wvb_ref[...], preferred_element_type=F32)

    def conv(u_ref, cw_ref, cb_ref):
        acc = cb_ref[...]
        for tap in range(CONV_WIDTH):
            lo = CARRY_ROWS - (CONV_WIDTH - 1) + tap
            acc = acc + u_ref[lo:lo + bm, :] * cw_ref[tap:tap + 1, :]
        return acc

    g = conv(ug_ref, cwg_ref, cbg_ref)
    v = conv(uv_ref, cwv_ref, cbv_ref)
    o_ref[...] = (jax.nn.silu(g) * v).astype(o_ref.dtype)

    ug_ref[0:CARRY_ROWS, :] = ug_ref[bm:bm + CARRY_ROWS, :]
    uv_ref[0:CARRY_ROWS, :] = uv_ref[bm:bm + CARRY_ROWS, :]


def _up_proj(h2, w_up, conv_w, conv_b, w_down, seq, bm=1024, bn=512):
    m, k = h2.shape
    f = w_up.shape[1] // 2
    nt = f // bn
    mt = m // bm
    assert nt * mt * CAST_ROWS == w_down.shape[0]
    kern = functools.partial(_upproj_kernel, tiles_per_seq=seq // bm)
    cb = conv_b.reshape(1, 2 * f)
    slab = lambda j, i: (j * mt + i, 0)
    return pl.pallas_call(
        kern,
        grid=(nt, mt),
        in_specs=[pl.BlockSpec((bm, k), lambda j, i: (i, 0)),
                  pl.BlockSpec((k, bn), lambda j, i: (0, j)),
                  pl.BlockSpec((k, bn), lambda j, i: (0, j + nt)),
                  pl.BlockSpec((CONV_WIDTH, bn), lambda j, i: (0, j)),
                  pl.BlockSpec((CONV_WIDTH, bn), lambda j, i: (0, j + nt)),
                  pl.BlockSpec((1, bn), lambda j, i: (0, j)),
                  pl.BlockSpec((1, bn), lambda j, i: (0, j + nt)),
                  pl.BlockSpec((CAST_ROWS, w_down.shape[1]), slab)],
        out_specs=[pl.BlockSpec((bm, bn), lambda j, i: (i, j)),
                   pl.BlockSpec((CAST_ROWS, w_down.shape[1]), slab)],
        out_shape=[jax.ShapeDtypeStruct((m, f), BF16),
                   jax.ShapeDtypeStruct(w_down.shape, BF16)],
        scratch_shapes=[pltpu.VMEM((k, bn), BF16), pltpu.VMEM((k, bn), BF16),
                        pltpu.VMEM((bm + CARRY_ROWS, bn), F32),
                        pltpu.VMEM((bm + CARRY_ROWS, bn), F32)],
        compiler_params=_params(2),
        name="up_proj",
    )(h2, w_up, w_up, conv_w, conv_w, cb, cb, w_down)


def _downproj_kernel(a_ref, w_ref, x1_ref, g_ref, o_ref):
    for rows in _row_groups(a_ref.shape[0]):
        x2 = x1_ref[rows, :] + jnp.dot(a_ref[rows, :], w_ref[...],
                                       preferred_element_type=F32)
        o_ref[rows, :] = _rms(x2, g_ref[...])


def _down_proj(act, w_down_b, x1, g, bm=512):
    m, k = act.shape
    n = w_down_b.shape[1]
    return pl.pallas_call(
        _downproj_kernel,
        grid=(m // bm,),
        in_specs=[pl.BlockSpec((bm, k), lambda i: (i, 0)),
                  _resident((k, n), lambda i: (0, 0)),
                  pl.BlockSpec((bm, n), lambda i: (i, 0)),
                  pl.BlockSpec((1, n), lambda i: (0, 0))],
        out_specs=pl.BlockSpec((bm, n), lambda i: (i, 0)),
        out_shape=jax.ShapeDtypeStruct((m, n), F32),
        compiler_params=_params(1),
        name="down_proj",
    )(act, w_down_b, x1, g.reshape(1, n))


def _rotary_tables(seq):
    half = RET_QK_DIM // 2
    inv_freq = ROPE_BASE ** (-jnp.arange(0, RET_QK_DIM, 2, dtype=F32) / RET_QK_DIM)
    ang = jnp.arange(seq, dtype=F32)[:, None] * inv_freq[None, :]
    cos = jnp.cos(ang)
    sin = jnp.sin(ang)
    cos_t = jnp.tile(cos, (1, LANES // half))
    sin_t = jnp.tile(jnp.concatenate([-sin, sin], axis=1), (1, LANES // RET_QK_DIM))
    return cos_t, sin_t


def kernel(x, norm1_g, w_in, sgu_ln_g, sgu_ln_b, sgu_w_s, sgu_b_s, ret_gn_g,
           w_out, norm2_g, w_up, conv_w, conv_b, w_down, final_g):
    bsz, seq, d = x.shape
    assert w_in.shape[0] == 1, "single-layer block only"
    x2d = x.reshape(bsz * seq, d)
    cos_t, sin_t = _rotary_tables(seq)
    h1, zu = _in_proj_u(x2d, norm1_g[0], w_in[0])
    z, w_out_b = _in_proj(h1, w_in[0], sgu_ln_g[0], sgu_ln_b[0], cos_t, sin_t, w_out[0], seq)
    mix = _mixer(zu, z, sgu_w_s[0], sgu_b_s[0], ret_gn_g[0], seq)
    x1, h2 = _out_proj(mix, w_out_b, x2d, norm2_g[0])
    act, w_down_b = _up_proj(h2, w_up[0], conv_w[0], conv_b[0], w_down[0], seq)
    out = _down_proj(act, w_down_b, x1, final_g)
    return out.reshape(bsz, seq, d)
```

```python
import functools

import jax
import jax.numpy as jnp
from jax import lax
from jax.experimental import pallas as pl
from jax.experimental.pallas import tpu as pltpu

F32 = jnp.float32
BF16 = jnp.bfloat16

D_MODEL = 2048
CHUNK = 64
SGU_WIDTH = 1024
SGU_HEADS = 8
SGU_BLOCK = 128
RET_WIDTH = 1024
RET_HEADS = 8
RET_V_DIM = 128
RET_QK_DIM = 64
RET_QK_WIDTH = 512
ROPE_BASE = 10000.0
D_FF = 5632
CONV_WIDTH = 3
EPS = 1e-6
LANES = 128
COL_TILE = 1024

VMEM_LIMIT_BYTES = 56 * 1024 * 1024
SUB_ROWS = 512
CAST_ROWS = 64


def _params(n_axes):
    return pltpu.CompilerParams(
        dimension_semantics=("arbitrary",) * n_axes,
        vmem_limit_bytes=VMEM_LIMIT_BYTES,
    )


def _resident(block_shape, index_map):
    return pl.BlockSpec(block_shape, index_map, pipeline_mode=pl.Buffered(1))


def _rms(x, g):
    ms = jnp.mean(x * x, axis=-1, keepdims=True)
    return x * lax.rsqrt(ms + EPS) * g


def _row_groups(n_rows):
    return [slice(r, r + SUB_ROWS) for r in range(0, n_rows, SUB_ROWS)]


def _inproj_u_kernel(x_ref, g_ref, w_ref, h1_ref, o_ref, wb_ref):
    @pl.when(pl.program_id(0) == 0)
    def _():
        wb_ref[...] = w_ref[...].astype(BF16)

    for rows in _row_groups(x_ref.shape[0]):
        h = _rms(x_ref[rows, :], g_ref[...]).astype(BF16)
        h1_ref[rows, :] = h
        acc = jnp.dot(h, wb_ref[...], preferred_element_type=F32)
        o_ref[rows, :] = jax.nn.gelu(acc).astype(o_ref.dtype)


def _in_proj_u(x2d, g1, w_in, bm=1024):
    m, d = x2d.shape
    return pl.pallas_call(
        _inproj_u_kernel,
        grid=(m // bm,),
        in_specs=[pl.BlockSpec((bm, d), lambda i: (i, 0)),
                  pl.BlockSpec((1, d), lambda i: (0, 0)),
                  _resident((d, COL_TILE), lambda i: (0, 0))],
        out_specs=[pl.BlockSpec((bm, d), lambda i: (i, 0)),
                   pl.BlockSpec((bm, COL_TILE), lambda i: (i, 0))],
        out_shape=[jax.ShapeDtypeStruct((m, d), BF16),
                   jax.ShapeDtypeStruct((m, COL_TILE), BF16)],
        scratch_shapes=[pltpu.VMEM((d, COL_TILE), BF16)],
        compiler_params=_params(1),
        name="in_proj_u",
    )(x2d, g1.reshape(1, d), w_in)


def _inproj_kernel(h_ref, w_ref, lng_ref, lnb_ref, cos_ref, sin_ref, wo_ref,
                   o_ref, wob_ref, wb_ref):
    j = pl.program_id(0)

    @pl.when(pl.program_id(1) == 0)
    def _():
        wb_ref[...] = w_ref[...].astype(BF16)

    wob_ref[...] = wo_ref[...].astype(BF16)

    def ln_gelu(acc, rows):
        a = jax.nn.gelu(acc)
        mu = jnp.mean(a, axis=-1, keepdims=True)
        c = a - mu
        var = jnp.mean(c * c, axis=-1, keepdims=True)
        return c * lax.rsqrt(var + EPS) * lng_ref[...] + lnb_ref[...]

    def rotary(acc, rows):
        cos = cos_ref[rows, :]
        sin = sin_ref[rows, :]
        lane = lax.broadcasted_iota(jnp.int32, cos.shape, 1)
        first_half = (lane % RET_QK_DIM) < (RET_QK_DIM // 2)
        n_col = COL_TILE // LANES
        out = []
        for c in range(n_col):
            xc = acc[:, c * LANES:(c + 1) * LANES]
            partner = jnp.where(first_half,
                                pltpu.roll(xc, LANES - RET_QK_DIM // 2, 1),
                                pltpu.roll(xc, RET_QK_DIM // 2, 1))
            r = xc * cos + partner * sin
            if c >= n_col // 2:
                r = r * (RET_QK_DIM ** -0.5)
            out.append(r)
        return jnp.concatenate(out, axis=1)

    epilogues = (ln_gelu, rotary, lambda acc, rows: acc,
                 lambda acc, rows: jax.nn.silu(acc))

    for t, epilogue in enumerate(epilogues):
        @pl.when(j == t)
        def _(epilogue=epilogue):
            for rows in _row_groups(h_ref.shape[0]):
                acc = jnp.dot(h_ref[rows, :], wb_ref[...], preferred_element_type=F32)
                o_ref[rows, :] = epilogue(acc, rows).astype(o_ref.dtype)


def _in_proj(h1, w_in, ln_g, ln_b, cos_t, sin_t, w_out, seq, bm=1024):
    m, k = h1.shape
    nt = w_in.shape[1] // COL_TILE - 1
    mt = m // bm
    seq_tiles = seq // bm
    assert nt * mt * CAST_ROWS == w_out.shape[0]
    slab = lambda j, i: (j * mt + i, 0)
    return pl.pallas_call(
        _inproj_kernel,
        grid=(nt, mt),
        in_specs=[pl.BlockSpec((bm, k), lambda j, i: (i, 0)),
                  pl.BlockSpec((k, COL_TILE), lambda j, i: (0, j + 1)),
                  pl.BlockSpec((1, COL_TILE), lambda j, i: (0, 0)),
                  pl.BlockSpec((1, COL_TILE), lambda j, i: (0, 0)),
                  pl.BlockSpec((bm, LANES), lambda j, i: (i % seq_tiles, 0)),
                  pl.BlockSpec((bm, LANES), lambda j, i: (i % seq_tiles, 0)),
                  pl.BlockSpec((CAST_ROWS, w_out.shape[1]), slab)],
        out_specs=[pl.BlockSpec((bm, COL_TILE), lambda j, i: (i, j)),
                   pl.BlockSpec((CAST_ROWS, w_out.shape[1]), slab)],
        out_shape=[jax.ShapeDtypeStruct((m, nt * COL_TILE), BF16),
                   jax.ShapeDtypeStruct(w_out.shape, BF16)],
        scratch_shapes=[pltpu.VMEM((k, COL_TILE), BF16)],
        compiler_params=_params(2),
        name="in_proj",
    )(h1, w_in, ln_g.reshape(1, COL_TILE), ln_b.reshape(1, COL_TILE), cos_t, sin_t, w_out)


RET_BLOCK = 256


def _mixer_kernel(gu_ref, vn_ref, qk_ref, vr_ref, gate_ref, ws_ref, bs_ref,
                  dmat_ref, qdec_ref, kdec_ref, cdec_ref, gng_ref,
                  o_ref, state_ref, *, steps_per_seq, blocks_per_step):
    step = pl.program_id(0)

    @pl.when(step % steps_per_seq == 0)
    def _():
        state_ref[...] = jnp.zeros_like(state_ref)

    t = SGU_BLOCK
    rb = RET_BLOCK
    row_i = lax.broadcasted_iota(jnp.int32, (t, t), 0)
    col_i = lax.broadcasted_iota(jnp.int32, (t, t), 1)
    sgu_mask = (row_i // CHUNK) >= (col_i // CHUNK)
    lane_lo = lax.broadcasted_iota(jnp.int32, (rb, LANES), 1) < RET_QK_DIM
    st_row = lax.broadcasted_iota(jnp.int32, (LANES, 2 * RET_V_DIM), 0) < RET_QK_DIM
    st_col = lax.broadcasted_iota(jnp.int32, (LANES, 2 * RET_V_DIM), 1) < RET_V_DIM
    own_block = st_row == st_col

    for b in range(blocks_per_step):
        r0 = b * rb
        rows = slice(r0, r0 + rb)
        ra = slice(r0, r0 + t)
        rc = slice(r0 + t, r0 + 2 * t)

        for h in range(SGU_HEADS):
            cols = slice(h * t, (h + 1) * t)
            w = jnp.where(sgu_mask, ws_ref[h], jnp.zeros((), ws_ref.dtype))
            vb = jnp.concatenate([vn_ref[ra, cols], vn_ref[rc, cols]], axis=1)
            mixed = jnp.dot(w, vb, preferred_element_type=F32) + bs_ref[h]
            o_ref[ra, cols] = (gu_ref[ra, cols].astype(F32) * mixed[:, :t]).astype(o_ref.dtype)
            o_ref[rc, cols] = (gu_ref[rc, cols].astype(F32) * mixed[:, t:]).astype(o_ref.dtype)

        for p in range(RET_HEADS // 2):
            qp = qk_ref[rows, p * LANES:(p + 1) * LANES]
            kp = qk_ref[rows, RET_QK_WIDTH + p * LANES:RET_QK_WIDTH + (p + 1) * LANES]
            vp = vr_ref[rows, 2 * p * RET_V_DIM:2 * (p + 1) * RET_V_DIM]
            st = state_ref[p]
            qd = (qp.astype(F32) * qdec_ref[p]).astype(BF16)
            cross = jnp.dot(qd, st.astype(BF16), preferred_element_type=F32)
            for e in range(2):
                h = 2 * p + e
                keep = lane_lo if e == 0 else jnp.logical_not(lane_lo)
                qm = jnp.where(keep, qp, jnp.zeros((), qp.dtype))
                s = lax.dot_general(qm, kp, (((1,), (1,)), ((), ())),
                                    preferred_element_type=F32)
                s = (s * dmat_ref[h]).astype(BF16)
                hv = slice(e * RET_V_DIM, (e + 1) * RET_V_DIM)
                y = jnp.dot(s, vp[:, hv], preferred_element_type=F32) + cross[:, hv]
                mu = jnp.mean(y, axis=-1, keepdims=True)
                c = y - mu
                var = jnp.mean(c * c, axis=-1, keepdims=True)
                hc = slice(h * RET_V_DIM, (h + 1) * RET_V_DIM)
                yn = c * lax.rsqrt(var + EPS) * gng_ref[:, hc]
                oc = slice(SGU_WIDTH + h * RET_V_DIM, SGU_WIDTH + (h + 1) * RET_V_DIM)
                o_ref[rows, oc] = (gate_ref[rows, hc].astype(F32) * yn).astype(o_ref.dtype)
            kd = (kp.astype(F32) * kdec_ref[p]).astype(BF16)
            kv = lax.dot_general(kd, vp, (((0,), (0,)), ((), ())),
                                 preferred_element_type=F32)
            state_ref[p] = st * cdec_ref[p] + jnp.where(own_block, kv, 0.0)


def _retention_tables():
    t = RET_BLOCK
    gammas = 1.0 - jnp.exp2(-5.0 - jnp.arange(RET_HEADS, dtype=F32))
    log_g = jnp.log(gammas)
    idx = jnp.arange(t, dtype=F32)
    n = idx[:, None]
    m = idx[None, :]
    cn = jnp.floor(n / CHUNK)
    cm = jnp.floor(m / CHUNK)
    dist = jnp.where(cn == cm, jnp.abs(n - m), n - m)
    dec = jnp.exp(log_g[:, None, None] * dist[None])
    dmat = jnp.where((cn >= cm)[None], dec, 0.0)
    q_dec = jnp.exp(log_g[:, None] * (idx + 1.0)[None, :])
    k_dec = jnp.exp(log_g[:, None] * (t - 1.0 - idx)[None, :])
    c_dec = jnp.exp(log_g * t)

    def lanes_by_pair(per_head_rows):
        a = per_head_rows.reshape(RET_HEADS // 2, 2, t)
        return jnp.repeat(jnp.transpose(a, (0, 2, 1)), RET_QK_DIM, axis=2)

    qdec = lanes_by_pair(q_dec)
    kdec = lanes_by_pair(k_dec)
    cdec = jnp.repeat(c_dec.reshape(RET_HEADS // 2, 2), RET_QK_DIM, axis=1)
    cdec = jnp.broadcast_to(cdec[:, :, None], (RET_HEADS // 2, LANES, 2 * RET_V_DIM))
    return dmat, qdec, kdec, cdec


def _mixer(zu, z, w_s, b_s, gn_g, seq, tb=1024):
    m = z.shape[0]
    t = SGU_BLOCK
    rb = RET_BLOCK
    dmat, qdec, kdec, cdec = _retention_tables()
    bias = jnp.broadcast_to(b_s[:, :, None], (SGU_HEADS, t, 2 * t))
    const3 = lambda s: (0, 0, 0)
    kern = functools.partial(_mixer_kernel, steps_per_seq=seq // tb,
                             blocks_per_step=tb // rb)
    zspec = lambda c: pl.BlockSpec((tb, COL_TILE), lambda s, c=c: (s, c))
    return pl.pallas_call(
        kern,
        grid=(m // tb,),
        in_specs=[zspec(0), zspec(0), zspec(1), zspec(2), zspec(3),
                  pl.BlockSpec((SGU_HEADS, t, t), const3),
                  pl.BlockSpec((SGU_HEADS, t, 2 * t), const3),
                  pl.BlockSpec((RET_HEADS, rb, rb), const3),
                  pl.BlockSpec((RET_HEADS // 2, rb, LANES), const3),
                  pl.BlockSpec((RET_HEADS // 2, rb, LANES), const3),
                  pl.BlockSpec((RET_HEADS // 2, LANES, 2 * RET_V_DIM), const3),
                  pl.BlockSpec((1, RET_WIDTH), lambda s: (0, 0))],
        out_specs=pl.BlockSpec((tb, SGU_WIDTH + RET_WIDTH), lambda s: (s, 0)),
        out_shape=jax.ShapeDtypeStruct((m, SGU_WIDTH + RET_WIDTH), BF16),
        scratch_shapes=[pltpu.VMEM((RET_HEADS // 2, LANES, 2 * RET_V_DIM), F32)],
        compiler_params=_params(1),
        name="mixer",
    )(zu, z, z, z, z, w_s.astype(BF16), bias, dmat, qdec, kdec, cdec,
      gn_g.reshape(1, RET_WIDTH))


def _outproj_kernel(mix_ref, w_ref, x_ref, g_ref, x1_ref, h2_ref):
    for rows in _row_groups(mix_ref.shape[0]):
        x1 = x_ref[rows, :] + jnp.dot(mix_ref[rows, :], w_ref[...],
                                      preferred_element_type=F32)
        x1_ref[rows, :] = x1
        h2_ref[rows, :] = _rms(x1, g_ref[...]).astype(h2_ref.dtype)


def _out_proj(mix, w_out_b, x2d, g2, bm=512):
    m, k = mix.shape
    n = w_out_b.shape[1]
    return pl.pallas_call(
        _outproj_kernel,
        grid=(m // bm,),
        in_specs=[pl.BlockSpec((bm, k), lambda i: (i, 0)),
                  _resident((k, n), lambda i: (0, 0)),
                  pl.BlockSpec((bm, n), lambda i: (i, 0)),
                  pl.BlockSpec((1, n), lambda i: (0, 0))],
        out_specs=[pl.BlockSpec((bm, n), lambda i: (i, 0)),
                   pl.BlockSpec((bm, n), lambda i: (i, 0))],
        out_shape=[jax.ShapeDtypeStruct((m, n), F32),
                   jax.ShapeDtypeStruct((m, n), BF16)],
        compiler_params=_params(1),
        name="out_proj",
    )(mix, w_out_b, x2d, g2.reshape(1, n))


CARRY_ROWS = 8
ROW_PITCH = 2


def _upproj_kernel(h_ref, wg_ref, wv_ref, cwg_ref, cwv_ref, cbg_ref, cbv_ref, wd_ref,
                   o_ref, wdb_ref, wgb_ref, wvb_ref, ug_ref, uv_ref, *, tiles_per_seq):
    i = pl.program_id(1)
    bm = h_ref.shape[0]
    n_slab = ug_ref.shape[0]

    @pl.when(i == 0)
    def _():
        wgb_ref[...] = wg_ref[...].astype(BF16)
        wvb_ref[...] = wv_ref[...].astype(BF16)

    def rows(lo, n):
        return pl.ds(ROW_PITCH * lo, n, stride=ROW_PITCH)

    @pl.when(i % tiles_per_seq == 0)
    def _():
        for u_ref in (ug_ref, uv_ref):
            u_ref[:, 0:ROW_PITCH * CARRY_ROWS, :] = jnp.zeros(
                (n_slab, ROW_PITCH * CARRY_ROWS, LANES), F32)

    wdb_ref[...] = wd_ref[...].astype(BF16)

    h = h_ref[...]

    def matmul_to(u_ref, w_ref):
        res = jnp.dot(h, w_ref[...], preferred_element_type=F32)
        slabs = [res[:, c * LANES:(c + 1) * LANES] for c in range(n_slab)]
        for c in range(n_slab):
            u_ref[c, rows(CARRY_ROWS, bm), :] = slabs[c]
        return slabs

    def conv(u_ref, cur, cw_ref, cb_ref, c):
        cols = slice(c * LANES, (c + 1) * LANES)
        acc = cb_ref[:, cols] + cur * cw_ref[CONV_WIDTH - 1:CONV_WIDTH, cols]
        for tap in range(CONV_WIDTH - 1):
            lo = CARRY_ROWS - (CONV_WIDTH - 1) + tap
            acc = acc + u_ref[c, rows(lo, bm), :] * cw_ref[tap:tap + 1, cols]
        return acc

    g_slabs = matmul_to(ug_ref, wgb_ref)
    gate = [jax.nn.silu(conv(ug_ref, g_slabs[c], cwg_ref, cbg_ref, c)) for c in range(n_slab)]
    v_slabs = matmul_to(uv_ref, wvb_ref)
    for c in range(n_slab):
        v = conv(uv_ref, v_slabs[c], cwv_ref, cbv_ref, c)
        o_ref[:, c * LANES:(c + 1) * LANES] = (gate[c] * v).astype(o_ref.dtype)

    for u_ref in (ug_ref, uv_ref):
        for c in range(n_slab):
            u_ref[c, rows(0, CARRY_ROWS), :] = u_ref[c, rows(bm, CARRY_ROWS), :]


def _up_proj(h2, w_up, conv_w, conv_b, w_down, seq, bm=1024, bn=512):
    m, k = h2.shape
    f = w_up.shape[1] // 2
    nt = f // bn
    mt = m // bm
    cast_rows = w_down.shape[0] // (nt * mt)
    assert nt * mt * cast_rows == w_down.shape[0] and cast_rows % 16 == 0
    kern = functools.partial(_upproj_kernel, tiles_per_seq=seq // bm)
    cb = conv_b.reshape(1, 2 * f)
    slab = lambda j, i: (j * mt + i, 0)
    return pl.pallas_call(
        kern,
        grid=(nt, mt),
        in_specs=[pl.BlockSpec((bm, k), lambda j, i: (i, 0)),
                  pl.BlockSpec((k, bn), lambda j, i: (0, j)),
                  pl.BlockSpec((k, bn), lambda j, i: (0, j + nt)),
                  pl.BlockSpec((CONV_WIDTH, bn), lambda j, i: (0, j)),
                  pl.BlockSpec((CONV_WIDTH, bn), lambda j, i: (0, j + nt)),
                  pl.BlockSpec((1, bn), lambda j, i: (0, j)),
                  pl.BlockSpec((1, bn), lambda j, i: (0, j + nt)),
                  pl.BlockSpec((cast_rows, w_down.shape[1]), slab)],
        out_specs=[pl.BlockSpec((bm, bn), lambda j, i: (i, j)),
                   pl.BlockSpec((cast_rows, w_down.shape[1]), slab)],
        out_shape=[jax.ShapeDtypeStruct((m, f), BF16),
                   jax.ShapeDtypeStruct(w_down.shape, BF16)],
        scratch_shapes=[pltpu.VMEM((k, bn), BF16), pltpu.VMEM((k, bn), BF16),
                        pltpu.VMEM((bn // LANES, ROW_PITCH * (bm + CARRY_ROWS), LANES), F32),
                        pltpu.VMEM((bn // LANES, ROW_PITCH * (bm + CARRY_ROWS), LANES), F32)],
        compiler_params=_params(2),
        name="up_proj",
    )(h2, w_up, w_up, conv_w, conv_w, cb, cb, w_down)


def _downproj_kernel(a_ref, w_ref, x1_ref, g_ref, o_ref):
    for rows in _row_groups(a_ref.shape[0]):
        x2 = x1_ref[rows, :] + jnp.dot(a_ref[rows, :], w_ref[...],
                                       preferred_element_type=F32)
        o_ref[rows, :] = _rms(x2, g_ref[...])


def _down_proj(act, w_down_b, x1, g, bm=512):
    m, k = act.shape
    n = w_down_b.shape[1]
    return pl.pallas_call(
        _downproj_kernel,
        grid=(m // bm,),
        in_specs=[pl.BlockSpec((bm, k), lambda i: (i, 0)),
                  _resident((k, n), lambda i: (0, 0)),
                  pl.BlockSpec((bm, n), lambda i: (i, 0)),
                  pl.BlockSpec((1, n), lambda i: (0, 0))],
        out_specs=pl.BlockSpec((bm, n), lambda i: (i, 0)),
        out_shape=jax.ShapeDtypeStruct((m, n), F32),
        compiler_params=_params(1),
        name="down_proj",
    )(act, w_down_b, x1, g.reshape(1, n))


def _rotary_tables(seq):
    half = RET_QK_DIM // 2
    inv_freq = ROPE_BASE ** (-jnp.arange(0, RET_QK_DIM, 2, dtype=F32) / RET_QK_DIM)
    ang = jnp.arange(seq, dtype=F32)[:, None] * inv_freq[None, :]
    cos = jnp.cos(ang)
    sin = jnp.sin(ang)
    cos_t = jnp.tile(cos, (1, LANES // half))
    sin_t = jnp.tile(jnp.concatenate([-sin, sin], axis=1), (1, LANES // RET_QK_DIM))
    return cos_t, sin_t


def kernel(x, norm1_g, w_in, sgu_ln_g, sgu_ln_b, sgu_w_s, sgu_b_s, ret_gn_g,
           w_out, norm2_g, w_up, conv_w, conv_b, w_down, final_g):
    bsz, seq, d = x.shape
    assert w_in.shape[0] == 1, "single-layer block only"
    x2d = x.reshape(bsz * seq, d)
    cos_t, sin_t = _rotary_tables(seq)
    h1, zu = _in_proj_u(x2d, norm1_g[0], w_in[0])
    z, w_out_b = _in_proj(h1, w_in[0], sgu_ln_g[0], sgu_ln_b[0], cos_t, sin_t, w_out[0], seq)
    mix = _mixer(zu, z, sgu_w_s[0], sgu_b_s[0], ret_gn_g[0], seq)
    x1, h2 = _out_proj(mix, w_out_b, x2d, norm2_g[0])
    act, w_down_b = _up_proj(h2, w_up[0], conv_w[0], conv_b[0], w_down[0], seq)
    out = _down_proj(act, w_down_b, x1, final_g)
    return out.reshape(bsz, seq, d)
```

```python
import functools

import jax
import jax.numpy as jnp
from jax import lax
from jax.experimental import pallas as pl
from jax.experimental.pallas import tpu as pltpu

F32 = jnp.float32
BF16 = jnp.bfloat16

D_MODEL = 2048
CHUNK = 64
SGU_WIDTH = 1024
SGU_HEADS = 8
SGU_BLOCK = 128
RET_WIDTH = 1024
RET_HEADS = 8
RET_V_DIM = 128
RET_QK_DIM = 64
RET_QK_WIDTH = 512
ROPE_BASE = 10000.0
D_FF = 5632
CONV_WIDTH = 3
EPS = 1e-6
LANES = 128
COL_TILE = 1024

VMEM_LIMIT_BYTES = 56 * 1024 * 1024
SUB_ROWS = 512
CAST_ROWS = 64


def _params(n_axes):
    return pltpu.CompilerParams(
        dimension_semantics=("arbitrary",) * n_axes,
        vmem_limit_bytes=VMEM_LIMIT_BYTES,
    )


def _resident(block_shape, index_map):
    return pl.BlockSpec(block_shape, index_map, pipeline_mode=pl.Buffered(1))


def _rms(x, g):
    ms = jnp.mean(x * x, axis=-1, keepdims=True)
    return x * lax.rsqrt(ms + EPS) * g


def _row_groups(n_rows):
    return [slice(r, r + SUB_ROWS) for r in range(0, n_rows, SUB_ROWS)]


def _inproj_u_kernel(x_ref, g_ref, w_ref, h1_ref, o_ref, wb_ref):
    @pl.when(pl.program_id(0) == 0)
    def _():
        wb_ref[...] = w_ref[...].astype(BF16)

    for rows in _row_groups(x_ref.shape[0]):
        h = _rms(x_ref[rows, :], g_ref[...]).astype(BF16)
        h1_ref[rows, :] = h
        acc = jnp.dot(h, wb_ref[...], preferred_element_type=F32)
        o_ref[rows, :] = jax.nn.gelu(acc).astype(o_ref.dtype)


def _in_proj_u(x2d, g1, w_in, bm=1024):
    m, d = x2d.shape
    return pl.pallas_call(
        _inproj_u_kernel,
        grid=(m // bm,),
        in_specs=[pl.BlockSpec((bm, d), lambda i: (i, 0)),
                  pl.BlockSpec((1, d), lambda i: (0, 0)),
                  _resident((d, COL_TILE), lambda i: (0, 0))],
        out_specs=[pl.BlockSpec((bm, d), lambda i: (i, 0)),
                   pl.BlockSpec((bm, COL_TILE), lambda i: (i, 0))],
        out_shape=[jax.ShapeDtypeStruct((m, d), BF16),
                   jax.ShapeDtypeStruct((m, COL_TILE), BF16)],
        scratch_shapes=[pltpu.VMEM((d, COL_TILE), BF16)],
        compiler_params=_params(1),
        name="in_proj_u",
    )(x2d, g1.reshape(1, d), w_in)


def _inproj_kernel(h_ref, w_ref, lng_ref, lnb_ref, cos_ref, sin_ref, wo_ref,
                   o_ref, wob_ref, wb_ref):
    j = pl.program_id(0)

    @pl.when(pl.program_id(1) == 0)
    def _():
        wb_ref[...] = w_ref[...].astype(BF16)

    wob_ref[...] = wo_ref[...].astype(BF16)

    def ln_gelu(acc, rows):
        a = jax.nn.gelu(acc)
        mu = jnp.mean(a, axis=-1, keepdims=True)
        c = a - mu
        var = jnp.mean(c * c, axis=-1, keepdims=True)
        return c * lax.rsqrt(var + EPS) * lng_ref[...] + lnb_ref[...]

    def rotary(acc, rows):
        cos = cos_ref[rows, :]
        sin = sin_ref[rows, :]
        lane = lax.broadcasted_iota(jnp.int32, cos.shape, 1)
        first_half = (lane % RET_QK_DIM) < (RET_QK_DIM // 2)
        n_col = COL_TILE // LANES
        out = []
        for c in range(n_col):
            xc = acc[:, c * LANES:(c + 1) * LANES]
            partner = jnp.where(first_half,
                                pltpu.roll(xc, LANES - RET_QK_DIM // 2, 1),
                                pltpu.roll(xc, RET_QK_DIM // 2, 1))
            r = xc * cos + partner * sin
            if c >= n_col // 2:
                r = r * (RET_QK_DIM ** -0.5)
            out.append(r)
        return jnp.concatenate(out, axis=1)

    epilogues = (ln_gelu, rotary, lambda acc, rows: acc,
                 lambda acc, rows: jax.nn.silu(acc))

    for t, epilogue in enumerate(epilogues):
        @pl.when(j == t)
        def _(epilogue=epilogue):
            for rows in _row_groups(h_ref.shape[0]):
                acc = jnp.dot(h_ref[rows, :], wb_ref[...], preferred_element_type=F32)
                o_ref[rows, :] = epilogue(acc, rows).astype(o_ref.dtype)


def _in_proj(h1, w_in, ln_g, ln_b, cos_t, sin_t, w_out, seq, bm=1024):
    m, k = h1.shape
    nt = w_in.shape[1] // COL_TILE - 1
    mt = m // bm
    seq_tiles = seq // bm
    assert nt * mt * CAST_ROWS == w_out.shape[0]
    slab = lambda j, i: (j * mt + i, 0)
    return pl.pallas_call(
        _inproj_kernel,
        grid=(nt, mt),
        in_specs=[pl.BlockSpec((bm, k), lambda j, i: (i, 0)),
                  pl.BlockSpec((k, COL_TILE), lambda j, i: (0, j + 1)),
                  pl.BlockSpec((1, COL_TILE), lambda j, i: (0, 0)),
                  pl.BlockSpec((1, COL_TILE), lambda j, i: (0, 0)),
                  pl.BlockSpec((bm, LANES), lambda j, i: (i % seq_tiles, 0)),
                  pl.BlockSpec((bm, LANES), lambda j, i: (i % seq_tiles, 0)),
                  pl.BlockSpec((CAST_ROWS, w_out.shape[1]), slab)],
        out_specs=[pl.BlockSpec((bm, COL_TILE), lambda j, i: (i, j)),
                   pl.BlockSpec((CAST_ROWS, w_out.shape[1]), slab)],
        out_shape=[jax.ShapeDtypeStruct((m, nt * COL_TILE), BF16),
                   jax.ShapeDtypeStruct(w_out.shape, BF16)],
        scratch_shapes=[pltpu.VMEM((k, COL_TILE), BF16)],
        compiler_params=_params(2),
        name="in_proj",
    )(h1, w_in, ln_g.reshape(1, COL_TILE), ln_b.reshape(1, COL_TILE), cos_t, sin_t, w_out)


RET_BLOCK = 256


def _mixer_kernel(gu_ref, vn_ref, qk_ref, vr_ref, gate_ref, ws_ref, bs_ref,
                  dmat_ref, qdec_ref, kdec_ref, cdec_ref, gng_ref, wo_ref, x_ref, g2_ref,
                  x1_ref, h2_ref, state_ref, o_ref, *, steps_per_seq, blocks_per_step):
    step = pl.program_id(0)

    @pl.when(step % steps_per_seq == 0)
    def _():
        state_ref[...] = jnp.zeros_like(state_ref)

    t = SGU_BLOCK
    rb = RET_BLOCK
    row_i = lax.broadcasted_iota(jnp.int32, (t, t), 0)
    col_i = lax.broadcasted_iota(jnp.int32, (t, t), 1)
    sgu_mask = (row_i // CHUNK) >= (col_i // CHUNK)
    lane_lo = lax.broadcasted_iota(jnp.int32, (rb, LANES), 1) < RET_QK_DIM
    st_row = lax.broadcasted_iota(jnp.int32, (LANES, 2 * RET_V_DIM), 0) < RET_QK_DIM
    st_col = lax.broadcasted_iota(jnp.int32, (LANES, 2 * RET_V_DIM), 1) < RET_V_DIM
    own_block = st_row == st_col

    for b in range(blocks_per_step):
        r0 = b * rb
        rows = slice(r0, r0 + rb)
        ra = slice(r0, r0 + t)
        rc = slice(r0 + t, r0 + 2 * t)

        for h in range(SGU_HEADS):
            cols = slice(h * t, (h + 1) * t)
            w = jnp.where(sgu_mask, ws_ref[h], jnp.zeros((), ws_ref.dtype))
            vb = jnp.concatenate([vn_ref[ra, cols], vn_ref[rc, cols]], axis=1)
            mixed = jnp.dot(w, vb, preferred_element_type=F32) + bs_ref[h]
            o_ref[ra, cols] = (gu_ref[ra, cols].astype(F32) * mixed[:, :t]).astype(o_ref.dtype)
            o_ref[rc, cols] = (gu_ref[rc, cols].astype(F32) * mixed[:, t:]).astype(o_ref.dtype)

        for p in range(RET_HEADS // 2):
            qp = qk_ref[rows, p * LANES:(p + 1) * LANES]
            kp = qk_ref[rows, RET_QK_WIDTH + p * LANES:RET_QK_WIDTH + (p + 1) * LANES]
            vp = vr_ref[rows, 2 * p * RET_V_DIM:2 * (p + 1) * RET_V_DIM]
            st = state_ref[p]
            qd = (qp.astype(F32) * qdec_ref[p]).astype(BF16)
            cross = jnp.dot(qd, st.astype(BF16), preferred_element_type=F32)
            for e in range(2):
                h = 2 * p + e
                keep = lane_lo if e == 0 else jnp.logical_not(lane_lo)
                qm = jnp.where(keep, qp, jnp.zeros((), qp.dtype))
                s = lax.dot_general(qm, kp, (((1,), (1,)), ((), ())),
                                    preferred_element_type=F32)
                s = (s * dmat_ref[h]).astype(BF16)
                hv = slice(e * RET_V_DIM, (e + 1) * RET_V_DIM)
                y = jnp.dot(s, vp[:, hv], preferred_element_type=F32) + cross[:, hv]
                mu = jnp.mean(y, axis=-1, keepdims=True)
                c = y - mu
                var = jnp.mean(c * c, axis=-1, keepdims=True)
                hc = slice(h * RET_V_DIM, (h + 1) * RET_V_DIM)
                yn = c * lax.rsqrt(var + EPS) * gng_ref[:, hc]
                oc = slice(SGU_WIDTH + h * RET_V_DIM, SGU_WIDTH + (h + 1) * RET_V_DIM)
                o_ref[rows, oc] = (gate_ref[rows, hc].astype(F32) * yn).astype(o_ref.dtype)
            kd = (kp.astype(F32) * kdec_ref[p]).astype(BF16)
            kv = lax.dot_general(kd, vp, (((0,), (0,)), ((), ())),
                                 preferred_element_type=F32)
            state_ref[p] = st * cdec_ref[p] + jnp.where(own_block, kv, 0.0)

    x1 = x_ref[...] + jnp.dot(o_ref[...], wo_ref[...], preferred_element_type=F32)
    x1_ref[...] = x1
    h2_ref[...] = _rms(x1, g2_ref[...]).astype(h2_ref.dtype)


def _retention_tables():
    t = RET_BLOCK
    gammas = 1.0 - jnp.exp2(-5.0 - jnp.arange(RET_HEADS, dtype=F32))
    log_g = jnp.log(gammas)
    idx = jnp.arange(t, dtype=F32)
    n = idx[:, None]
    m = idx[None, :]
    cn = jnp.floor(n / CHUNK)
    cm = jnp.floor(m / CHUNK)
    dist = jnp.where(cn == cm, jnp.abs(n - m), n - m)
    dec = jnp.exp(log_g[:, None, None] * dist[None])
    dmat = jnp.where((cn >= cm)[None], dec, 0.0)
    q_dec = jnp.exp(log_g[:, None] * (idx + 1.0)[None, :])
    k_dec = jnp.exp(log_g[:, None] * (t - 1.0 - idx)[None, :])
    c_dec = jnp.exp(log_g * t)

    def lanes_by_pair(per_head_rows):
        a = per_head_rows.reshape(RET_HEADS // 2, 2, t)
        return jnp.repeat(jnp.transpose(a, (0, 2, 1)), RET_QK_DIM, axis=2)

    qdec = lanes_by_pair(q_dec)
    kdec = lanes_by_pair(k_dec)
    cdec = jnp.repeat(c_dec.reshape(RET_HEADS // 2, 2), RET_QK_DIM, axis=1)
    cdec = jnp.broadcast_to(cdec[:, :, None], (RET_HEADS // 2, LANES, 2 * RET_V_DIM))
    return dmat, qdec, kdec, cdec


def _mixer(zu, z, w_s, b_s, gn_g, w_out_b, x2d, g2, seq, tb=512):
    m = z.shape[0]
    d = x2d.shape[1]
    t = SGU_BLOCK
    rb = RET_BLOCK
    dmat, qdec, kdec, cdec = _retention_tables()
    bias = jnp.broadcast_to(b_s[:, :, None], (SGU_HEADS, t, 2 * t))
    const3 = lambda s: (0, 0, 0)
    kern = functools.partial(_mixer_kernel, steps_per_seq=seq // tb,
                             blocks_per_step=tb // rb)
    zspec = lambda c: pl.BlockSpec((tb, COL_TILE), lambda s, c=c: (s, c))
    return pl.pallas_call(
        kern,
        grid=(m // tb,),
        in_specs=[zspec(0), zspec(0), zspec(1), zspec(2), zspec(3),
                  _resident((SGU_HEADS, t, t), const3),
                  _resident((SGU_HEADS, t, 2 * t), const3),
                  _resident((RET_HEADS, rb, rb), const3),
                  _resident((RET_HEADS // 2, rb, LANES), const3),
                  _resident((RET_HEADS // 2, rb, LANES), const3),
                  _resident((RET_HEADS // 2, LANES, 2 * RET_V_DIM), const3),
                  pl.BlockSpec((1, RET_WIDTH), lambda s: (0, 0)),
                  _resident(w_out_b.shape, lambda s: (0, 0)),
                  pl.BlockSpec((tb, d), lambda s: (s, 0)),
                  pl.BlockSpec((1, d), lambda s: (0, 0))],
        out_specs=[pl.BlockSpec((tb, d), lambda s: (s, 0)),
                   pl.BlockSpec((tb, d), lambda s: (s, 0))],
        out_shape=[jax.ShapeDtypeStruct((m, d), F32),
                   jax.ShapeDtypeStruct((m, d), BF16)],
        scratch_shapes=[pltpu.VMEM((RET_HEADS // 2, LANES, 2 * RET_V_DIM), F32),
                        pltpu.VMEM((tb, SGU_WIDTH + RET_WIDTH), BF16)],
        compiler_params=_params(1),
        name="mixer_out_proj",
    )(zu, z, z, z, z, w_s.astype(BF16), bias, dmat, qdec, kdec, cdec,
      gn_g.reshape(1, RET_WIDTH), w_out_b, x2d, g2.reshape(1, d))


CARRY_ROWS = 8
ROW_PITCH = 2


def _upproj_kernel(h_ref, wg_ref, wv_ref, cwg_ref, cwv_ref, cbg_ref, cbv_ref, wd_ref,
                   o_ref, wdb_ref, wgb_ref, wvb_ref, ug_ref, uv_ref, *, tiles_per_seq):
    i = pl.program_id(1)
    bm = h_ref.shape[0]
    n_slab = ug_ref.shape[0]

    @pl.when(i == 0)
    def _():
        wgb_ref[...] = wg_ref[...].astype(BF16)
        wvb_ref[...] = wv_ref[...].astype(BF16)

    def rows(lo, n):
        return pl.ds(ROW_PITCH * lo, n, stride=ROW_PITCH)

    @pl.when(i % tiles_per_seq == 0)
    def _():
        for u_ref in (ug_ref, uv_ref):
            u_ref[:, 0:ROW_PITCH * CARRY_ROWS, :] = jnp.zeros(
                (n_slab, ROW_PITCH * CARRY_ROWS, LANES), F32)

    wdb_ref[...] = wd_ref[...].astype(BF16)

    h = h_ref[...]

    def matmul_to(u_ref, w_ref):
        res = jnp.dot(h, w_ref[...], preferred_element_type=F32)
        slabs = [res[:, c * LANES:(c + 1) * LANES] for c in range(n_slab)]
        for c in range(n_slab):
            u_ref[c, rows(CARRY_ROWS, bm), :] = slabs[c]
        return slabs

    def conv(u_ref, cur, cw_ref, cb_ref, c):
        cols = slice(c * LANES, (c + 1) * LANES)
        acc = cb_ref[:, cols] + cur * cw_ref[CONV_WIDTH - 1:CONV_WIDTH, cols]
        for tap in range(CONV_WIDTH - 1):
            lo = CARRY_ROWS - (CONV_WIDTH - 1) + tap
            acc = acc + u_ref[c, rows(lo, bm), :] * cw_ref[tap:tap + 1, cols]
        return acc

    g_slabs = matmul_to(ug_ref, wgb_ref)
    gate = [jax.nn.silu(conv(ug_ref, g_slabs[c], cwg_ref, cbg_ref, c)) for c in range(n_slab)]
    v_slabs = matmul_to(uv_ref, wvb_ref)
    for c in range(n_slab):
        v = conv(uv_ref, v_slabs[c], cwv_ref, cbv_ref, c)
        o_ref[:, c * LANES:(c + 1) * LANES] = (gate[c] * v).astype(o_ref.dtype)

    for u_ref in (ug_ref, uv_ref):
        for c in range(n_slab):
            u_ref[c, rows(0, CARRY_ROWS), :] = u_ref[c, rows(bm, CARRY_ROWS), :]


def _up_proj(h2, w_up, conv_w, conv_b, w_down, seq, bm=1024, bn=512):
    m, k = h2.shape
    f = w_up.shape[1] // 2
    nt = f // bn
    mt = m // bm
    cast_rows = w_down.shape[0] // (nt * mt)
    assert nt * mt * cast_rows == w_down.shape[0] and cast_rows % 16 == 0
    kern = functools.partial(_upproj_kernel, tiles_per_seq=seq // bm)
    cb = conv_b.reshape(1, 2 * f)
    slab = lambda j, i: (j * mt + i, 0)
    return pl.pallas_call(
        kern,
        grid=(nt, mt),
        in_specs=[pl.BlockSpec((bm, k), lambda j, i: (i, 0)),
                  pl.BlockSpec((k, bn), lambda j, i: (0, j)),
                  pl.BlockSpec((k, bn), lambda j, i: (0, j + nt)),
                  pl.BlockSpec((CONV_WIDTH, bn), lambda j, i: (0, j)),
                  pl.BlockSpec((CONV_WIDTH, bn), lambda j, i: (0, j + nt)),
                  pl.BlockSpec((1, bn), lambda j, i: (0, j)),
                  pl.BlockSpec((1, bn), lambda j, i: (0, j + nt)),
                  pl.BlockSpec((cast_rows, w_down.shape[1]), slab)],
        out_specs=[pl.BlockSpec((bm, bn), lambda j, i: (i, j)),
                   pl.BlockSpec((cast_rows, w_down.shape[1]), slab)],
        out_shape=[jax.ShapeDtypeStruct((m, f), BF16),
                   jax.ShapeDtypeStruct(w_down.shape, BF16)],
        scratch_shapes=[pltpu.VMEM((k, bn), BF16), pltpu.VMEM((k, bn), BF16),
                        pltpu.VMEM((bn // LANES, ROW_PITCH * (bm + CARRY_ROWS), LANES), F32),
                        pltpu.VMEM((bn // LANES, ROW_PITCH * (bm + CARRY_ROWS), LANES), F32)],
        compiler_params=_params(2),
        name="up_proj",
    )(h2, w_up, w_up, conv_w, conv_w, cb, cb, w_down)


def _downproj_kernel(a_ref, w_ref, x1_ref, g_ref, o_ref):
    for rows in _row_groups(a_ref.shape[0]):
        x2 = x1_ref[rows, :] + jnp.dot(a_ref[rows, :], w_ref[...],
                                       preferred_element_type=F32)
        o_ref[rows, :] = _rms(x2, g_ref[...])


def _down_proj(act, w_down_b, x1, g, bm=512):
    m, k = act.shape
    n = w_down_b.shape[1]
    return pl.pallas_call(
        _downproj_kernel,
        grid=(m // bm,),
        in_specs=[pl.BlockSpec((bm, k), lambda i: (i, 0)),
                  _resident((k, n), lambda i: (0, 0)),
                  pl.BlockSpec((bm, n), lambda i: (i, 0)),
                  pl.BlockSpec((1, n), lambda i: (0, 0))],
        out_specs=pl.BlockSpec((bm, n), lambda i: (i, 0)),
        out_shape=jax.ShapeDtypeStruct((m, n), F32),
        compiler_params=_params(1),
        name="down_proj",
    )(act, w_down_b, x1, g.reshape(1, n))


def _rotary_tables(seq):
    half = RET_QK_DIM // 2
    inv_freq = ROPE_BASE ** (-jnp.arange(0, RET_QK_DIM, 2, dtype=F32) / RET_QK_DIM)
    ang = jnp.arange(seq, dtype=F32)[:, None] * inv_freq[None, :]
    cos = jnp.cos(ang)
    sin = jnp.sin(ang)
    cos_t = jnp.tile(cos, (1, LANES // half))
    sin_t = jnp.tile(jnp.concatenate([-sin, sin], axis=1), (1, LANES // RET_QK_DIM))
    return cos_t, sin_t


def kernel(x, norm1_g, w_in, sgu_ln_g, sgu_ln_b, sgu_w_s, sgu_b_s, ret_gn_g,
           w_out, norm2_g, w_up, conv_w, conv_b, w_down, final_g):
    bsz, seq, d = x.shape
    assert w_in.shape[0] == 1, "single-layer block only"
    x2d = x.reshape(bsz * seq, d)
    cos_t, sin_t = _rotary_tables(seq)
    h1, zu = _in_proj_u(x2d, norm1_g[0], w_in[0])
    z, w_out_b = _in_proj(h1, w_in[0], sgu_ln_g[0], sgu_ln_b[0], cos_t, sin_t, w_out[0], seq)
    x1, h2 = _mixer(zu, z, sgu_w_s[0], sgu_b_s[0], ret_gn_g[0], w_out_b, x2d, norm2_g[0], seq)
    act, w_down_b = _up_proj(h2, w_up[0], conv_w[0], conv_b[0], w_down[0], seq)
    out = _down_proj(act, w_down_b, x1, final_g)
    return out.reshape(bsz, seq, d)
```
